```python
import math
import jax, jax.numpy as jnp
from jax import lax
import numpy as np


D_MODEL = 2048
BATCH = 4
SEQ = 2048
DEPTH = 4
DEC_BATCH = 8
DEC_SEQ = 8
PAST_LEN = 16384
PAGE_SIZE = 128

N_MIXERS = 2
N_CONV_LAYERS = (DEPTH + 1) // 2
N_ATTN_LAYERS = DEPTH // 2
CONV_WIDTH = 3
CONV_DIM = D_MODEL
HEAD_DIM = 128
N_HEADS = D_MODEL // (2 * HEAD_DIM)
ATTN_DIM = N_HEADS * 2 * HEAD_DIM
N_META = 16
N_BUCKETS = 32
MAX_EXACT = N_BUCKETS // 2
MAX_DISTANCE = 128
Q_BLOCK = 128
EPS = 1e-6
NEG_INF = -1e30

kernel_name = 'hybrid_shortconv_diffattn_step'


def rms_norm(x, w):
    xf = x.astype(jnp.float32)
    xf = xf * lax.rsqrt(jnp.mean(xf * xf, axis=-1, keepdims=True) + EPS)
    return (xf * w.astype(jnp.float32)).astype(x.dtype)


def relative_bucket(q_pos, k_pos):
    n = jnp.maximum(q_pos[:, None] - k_pos[None, :], 0)
    nf = jnp.maximum(n, 1).astype(jnp.float32)
    large = MAX_EXACT + (jnp.log(nf / MAX_EXACT) / math.log(MAX_DISTANCE / MAX_EXACT)
                         * (N_BUCKETS - MAX_EXACT)).astype(jnp.int32)
    large = jnp.minimum(large, N_BUCKETS - 1)
    return jnp.where(n < MAX_EXACT, n, large)


def short_conv_mixer(x, conv_state, w_in, conv_w, w_out):
    L = x.shape[1]
    proj = jnp.einsum('bld,de->ble', x, w_in)
    u, gate_b, gate_c, z = jnp.split(proj, 4, axis=-1)
    cu = gate_c * u
    padded = jnp.concatenate([conv_state.astype(cu.dtype), cu], axis=1)
    conv = (padded[:, 0:L] * conv_w[0] + padded[:, 1:L + 1] * conv_w[1]
            + padded[:, 2:L + 2] * conv_w[2])
    y = gate_b * conv * jax.nn.silu(z)
    out = jnp.einsum('blc,cd->bld', y, w_out)
    return out, padded[:, L:]


def diff_attention_core(q, k, v, q_pos, k_pos, lam, rel_bias):
    b, Lq = q.shape[0], q.shape[1]
    blk = min(Q_BLOCK, Lq)
    nb = -(-Lq // blk)
    pad = nb * blk - Lq
    qp = jnp.pad(q, ((0, 0), (0, pad), (0, 0), (0, 0), (0, 0)))
    pos = jnp.pad(q_pos, (0, pad), mode='edge')
    qb = qp.reshape(b, nb, blk, N_HEADS, 2, HEAD_DIM).transpose(1, 0, 2, 3, 4, 5)
    pb = pos.reshape(nb, blk)
    kf = k.astype(jnp.float32)
    vf = v.astype(jnp.float32)
    scale = HEAD_DIM ** -0.5

    def block(args):
        qblk, pblk = args
        s = jnp.einsum('bqhcd,bkhcd->bhcqk', qblk.astype(jnp.float32), kf) * scale
        bias = rel_bias.astype(jnp.float32)[relative_bucket(pblk, k_pos)]
        bias = bias.transpose(2, 0, 1)[None, :, None]
        mask = k_pos[None, :] <= pblk[:, None]
        s = jnp.where(mask, s + bias, NEG_INF)
        p = jax.nn.softmax(s, axis=-1)
        a = p[:, :, 0] - lam * p[:, :, 1]
        return jnp.einsum('bhqk,bkhe->bqhe', a, vf)

    out = lax.map(block, (qb, pb))
    out = out.transpose(1, 0, 2, 3, 4).reshape(b, nb * blk, N_HEADS, 2 * HEAD_DIM)
    return out[:, :Lq]


def diff_attn_mixer(x, past_k, past_v, q_pos, k_pos, w_in, lam_params, lam_init,
                    subln_w, rel_bias, w_out):
    b, L, _ = x.shape
    proj = jnp.einsum('bld,de->ble', x, w_in)
    q, k, v, z = jnp.split(proj, 4, axis=-1)
    q = q.reshape(b, L, N_HEADS, 2, HEAD_DIM)
    k_rows = k.reshape(b, L, N_HEADS, 2 * HEAD_DIM)
    v_rows = v.reshape(b, L, N_HEADS, 2 * HEAD_DIM)
    if past_k is None:
        k_all, v_all = k_rows, v_rows
    else:
        k_all = jnp.concatenate([past_k.astype(k_rows.dtype), k_rows], axis=1)
        v_all = jnp.concatenate([past_v.astype(v_rows.dtype), v_rows], axis=1)
    lp = lam_params.astype(jnp.float32)
    lam = jnp.exp(jnp.sum(lp[0] * lp[1])) - jnp.exp(jnp.sum(lp[2] * lp[3])) + lam_init
    o = diff_attention_core(q, k_all.reshape(b, -1, N_HEADS, 2, HEAD_DIM), v_all,
                            q_pos, k_pos, lam, rel_bias)
    o = o * lax.rsqrt(jnp.mean(o * o, axis=-1, keepdims=True) + EPS)
    o = o * subln_w.astype(jnp.float32) * (1.0 - lam_init)
    y = o.reshape(b, L, ATTN_DIM).astype(x.dtype) * jax.nn.silu(z)
    out = jnp.einsum('ble,ed->bld', y, w_out)
    return out, k_rows, v_rows


def setup_inputs(seed: int = 0) -> dict:
    key = jax.random.key(seed)
    ks = jax.random.split(key, 20)
    f32 = jnp.float32
    n_pages = PAST_LEN // PAGE_SIZE
    n_used = DEC_BATCH * n_pages
    n_pool = n_used + max(1, n_used // 4)
    page_table = jax.random.permutation(ks[0], n_pool)[:n_used].astype(jnp.int32).reshape(DEC_BATCH, n_pages)
    row = (N_HEADS, 2 * HEAD_DIM)
    return {
        'x_prompt': jax.random.normal(ks[1], (BATCH, SEQ, D_MODEL), f32),
        'x_sample': jax.random.normal(ks[2], (DEC_BATCH, DEC_SEQ, D_MODEL), f32),
        'state_conv': jax.random.normal(ks[3], (N_CONV_LAYERS, DEC_BATCH, CONV_WIDTH - 1, CONV_DIM), f32),
        'cache_k': jax.random.normal(ks[4], (N_ATTN_LAYERS, n_pool, PAGE_SIZE) + row, f32),
        'cache_v': jax.random.normal(ks[5], (N_ATTN_LAYERS, n_pool, PAGE_SIZE) + row, f32),
        'page_table': page_table,
        'meta_tokens': jax.random.normal(ks[6], (N_META, D_MODEL), f32),
        'rel_bias': 0.3 * jax.random.normal(ks[7], (N_BUCKETS, N_HEADS), f32),
        'norm_w': 1.0 + 0.02 * jax.random.normal(ks[8], (DEPTH, D_MODEL), f32),
        'final_norm_w': 1.0 + 0.02 * jax.random.normal(ks[9], (D_MODEL,), f32),
        'conv_w_in': jax.random.normal(ks[10], (N_CONV_LAYERS, D_MODEL, 4 * CONV_DIM), f32) * D_MODEL ** -0.5,
        'conv_w': jax.random.normal(ks[11], (N_CONV_LAYERS, CONV_WIDTH, CONV_DIM), f32) * CONV_WIDTH ** -0.5,
        'conv_w_out': jax.random.normal(ks[12], (N_CONV_LAYERS, CONV_DIM, D_MODEL), f32) * CONV_DIM ** -0.5,
        'attn_w_in': jax.random.normal(ks[13], (N_ATTN_LAYERS, D_MODEL, 4 * ATTN_DIM), f32) * D_MODEL ** -0.5,
        'attn_lambda': 0.1 * jax.random.normal(ks[14], (N_ATTN_LAYERS, 4, HEAD_DIM), f32),
        'attn_subln_w': 1.0 + 0.02 * jax.random.normal(ks[15], (N_ATTN_LAYERS, 2 * HEAD_DIM), f32),
        'attn_w_out': jax.random.normal(ks[16], (N_ATTN_LAYERS, ATTN_DIM, D_MODEL), f32) * ATTN_DIM ** -0.5,
    }


def reference(x_prompt, x_sample, state_conv, cache_k, cache_v, page_table, meta_tokens,
              rel_bias, norm_w, final_norm_w, conv_w_in, conv_w, conv_w_out,
              attn_w_in, attn_lambda, attn_subln_w, attn_w_out):
    b, _, d = x_prompt.shape
    db, ls, _ = x_sample.shape
    meta = jnp.broadcast_to(meta_tokens.astype(x_prompt.dtype)[None], (b, N_META, d))
    hp = jnp.concatenate([meta, x_prompt], axis=1)
    hs = x_sample
    lp = hp.shape[1]
    n_pages = page_table.shape[1]
    past_len = n_pages * PAGE_SIZE
    pos_p = jnp.arange(lp, dtype=jnp.int32)
    pos_s_q = past_len + jnp.arange(ls, dtype=jnp.int32)
    pos_s_k = jnp.arange(past_len + ls, dtype=jnp.int32)

    conv_p, conv_s, k_p, v_p, k_s, v_s = [], [], [], [], [], []
    for i in range(DEPTH):
        xp_n = rms_norm(hp, norm_w[i])
        xs_n = rms_norm(hs, norm_w[i])
        j = i // N_MIXERS
        if i % N_MIXERS == 0:
            zero_state = jnp.zeros((b, CONV_WIDTH - 1, CONV_DIM), hp.dtype)
            yp, st_p = short_conv_mixer(xp_n, zero_state, conv_w_in[j], conv_w[j], conv_w_out[j])
            ys, st_s = short_conv_mixer(xs_n, state_conv[j], conv_w_in[j], conv_w[j], conv_w_out[j])
            conv_p.append(st_p)
            conv_s.append(st_s)
        else:
            lam_init = 0.8 - 0.6 * math.exp(-0.3 * i)
            past_k = cache_k[j, page_table].reshape(db, past_len, N_HEADS, 2 * HEAD_DIM)
            past_v = cache_v[j, page_table].reshape(db, past_len, N_HEADS, 2 * HEAD_DIM)
            yp, kp_rows, vp_rows = diff_attn_mixer(
                xp_n, None, None, pos_p, pos_p, attn_w_in[j], attn_lambda[j], lam_init,
                attn_subln_w[j], rel_bias, attn_w_out[j])
            ys, ks_rows, vs_rows = diff_attn_mixer(
                xs_n, past_k, past_v, pos_s_q, pos_s_k, attn_w_in[j], attn_lambda[j], lam_init,
                attn_subln_w[j], rel_bias, attn_w_out[j])
            k_p.append(kp_rows)
            v_p.append(vp_rows)
            k_s.append(ks_rows)
            v_s.append(vs_rows)
        hp = hp + yp
        hs = hs + ys

    y_prompt = rms_norm(hp, final_norm_w)[:, N_META:]
    y_sample = rms_norm(hs, final_norm_w)
    return (y_prompt, y_sample, jnp.stack(conv_p), jnp.stack(conv_s),
            jnp.stack(k_p), jnp.stack(v_p), jnp.stack(k_s), jnp.stack(v_s))
```

```python
import functools
import math

import jax
import jax.numpy as jnp
from jax import lax
from jax.experimental import pallas as pl
from jax.experimental.pallas import tpu as pltpu

N_META = 16
HEAD_DIM = 128
HEAD_WIDTH = 2 * HEAD_DIM
PAGE_SIZE = 128
N_BUCKETS = 32
MAX_EXACT = N_BUCKETS // 2
MAX_DISTANCE = 128
CONV_WIDTH = 3
EPS = 1e-6
NEG_INF = -1e30

BF16 = jnp.bfloat16
F32 = jnp.float32

V7X_LANES = 128
V7X_SUBLANES = 8
V7X_BF16_SUBLANES = 16
V7X_VMEM_LIMIT_BYTES = 56 * 1024 * 1024

ROW_TILE_CAP = 768
COL_TILE = 256
Q_TILE = 256
PAGES_PER_STEP = 4


def _row_tile(n_rows, cap):
    best = None
    for t in range(V7X_BF16_SUBLANES, min(n_rows, cap) + 1, V7X_BF16_SUBLANES):
        if n_rows % t == 0:
            best = t
    return n_rows if best is None else best


def _params(*semantics):
    return pltpu.CompilerParams(dimension_semantics=semantics, vmem_limit_bytes=V7X_VMEM_LIMIT_BYTES)


def _dot(a, b):
    return jnp.dot(a, b, preferred_element_type=F32)


def _dot_nt(a, b):
    return lax.dot_general(a, b, (((1,), (1,)), ((), ())), preferred_element_type=F32)


def _rms_norm(x, w):
    return x * lax.rsqrt(jnp.mean(x * x, axis=-1, keepdims=True) + EPS) * w


def _silu(z):
    return z * (1.0 / (1.0 + jnp.exp(-z)))


def _lambda(lp, lam_init):
    a = jnp.sum(lp[0:1] * lp[1:2], axis=-1, keepdims=True)
    b = jnp.sum(lp[2:3] * lp[3:4], axis=-1, keepdims=True)
    return jnp.exp(a) - jnp.exp(b) + lam_init


def _sub_norm_gate(o, sub_w, lam_init, z):
    o = o * lax.rsqrt(jnp.mean(o * o, axis=-1, keepdims=True) + EPS)
    o = o * sub_w * (1.0 - lam_init)
    return o * _silu(z)


def _div(x, n):
    return x >> (n.bit_length() - 1) if n & (n - 1) == 0 else lax.div(x, jnp.int32(n))


def _rem(x, n):
    return x & (n - 1) if n & (n - 1) == 0 else lax.rem(x, jnp.int32(n))


def _bias_tables_kernel(rb_ref, near_ref, new_ref, tail_ref, far_ref, *, n_heads):
    head = pl.program_id(0)

    def lookup(n):
        n = jnp.maximum(n, 0)
        nf = jnp.maximum(n, 1).astype(F32)
        large = MAX_EXACT + (
            jnp.log(nf / MAX_EXACT) / math.log(MAX_DISTANCE / MAX_EXACT) * (N_BUCKETS - MAX_EXACT)
        ).astype(jnp.int32)
        large = jnp.minimum(large, N_BUCKETS - 1)
        bucket = jnp.where(n < MAX_EXACT, n, large)
        out = jnp.zeros(n.shape, F32)
        for b in range(N_BUCKETS):
            out = jnp.where(bucket == b, rb_ref[b, head], out)
        return out

    shape = near_ref.shape[1:]
    near_ref[0] = lookup(lax.broadcasted_iota(jnp.int32, shape, 0) + MAX_DISTANCE
                         - lax.broadcasted_iota(jnp.int32, shape, 1))

    def paged(ref, bias_of_distance, offset):
        row = lax.broadcasted_iota(jnp.int32, ref.shape, 0)
        lane = lax.broadcasted_iota(jnp.int32, ref.shape, 1)
        bias = bias_of_distance(row + offset - _div(lane, n_heads))
        ref[...] = jnp.where(_rem(lane, n_heads) == head, bias, NEG_INF)

    paged(new_ref, lookup, 0)
    paged(tail_ref, lookup, PAGE_SIZE)
    paged(far_ref, lambda n: jnp.full(n.shape, rb_ref[N_BUCKETS - 1, head], F32), 0)


def _bias_tables(rel_bias, tq, ls):
    n_heads = rel_bias.shape[1]
    near_w = tq + MAX_DISTANCE
    page_rows = PAGE_SIZE * n_heads
    new_rows = -(-ls * n_heads // V7X_LANES) * V7X_LANES
    return pl.pallas_call(
        functools.partial(_bias_tables_kernel, n_heads=n_heads),
        grid=(n_heads,),
        in_specs=[pl.BlockSpec(memory_space=pltpu.SMEM)],
        out_specs=[
            pl.BlockSpec((1, tq, near_w), lambda h: (h, 0, 0)),
            pl.BlockSpec((ls, new_rows), lambda h: (h, 0)),
            pl.BlockSpec((ls, page_rows), lambda h: (h, 0)),
            pl.BlockSpec((ls, page_rows), lambda h: (h, 0)),
        ],
        out_shape=[
            jax.ShapeDtypeStruct((n_heads, tq, near_w), F32),
            jax.ShapeDtypeStruct((n_heads * ls, new_rows), F32),
            jax.ShapeDtypeStruct((n_heads * ls, page_rows), F32),
            jax.ShapeDtypeStruct((n_heads * ls, page_rows), F32),
        ],
        compiler_params=_params("arbitrary"),
        name="bias_tables",
    )(rel_bias)


def _gated_conv_chunk(xn, wu_ref, wb_ref, wc_ref, wz_ref, cw_ref, t, halo0, halo1):
    u = _dot(xn, wu_ref[...])
    gate_b = _dot(xn, wb_ref[...])
    gate_c = _dot(xn, wc_ref[...])
    z = _dot(xn, wz_ref[...])
    cu = gate_c * u
    prev1 = jnp.where(t >= 1, pltpu.roll(cu, 1, 0), halo1)
    prev2 = jnp.where(t >= 2, pltpu.roll(cu, 2, 0), jnp.where(t == 1, halo1, halo0))
    cw = cw_ref[...]
    conv = prev2 * cw[0:1] + prev1 * cw[1:2] + cu * cw[2:3]
    return cu, gate_b * conv * _silu(z)


def _conv_prompt_kernel(h_ref, nw_ref, wu_ref, wb_ref, wc_ref, wz_ref, cw_ref, wo_ref,
                        o_ref, tail_ref, xn_scr, carry_scr, *, tiles_per_seq):
    i, j = pl.program_id(0), pl.program_id(1)
    tm, tn = h_ref.shape[0], wu_ref.shape[1]

    @pl.when(j == 0)
    def _():
        h = h_ref[...]
        xn_scr[...] = _rms_norm(h, nw_ref[...]).astype(BF16)
        o_ref[...] = h

    @pl.when(jnp.logical_and(j == 0, i % tiles_per_seq == 0))
    def _():
        carry_scr[...] = jnp.zeros(carry_scr.shape, F32)

    t = lax.broadcasted_iota(jnp.int32, (tm, tn), 0)
    halo0 = carry_scr[j, V7X_SUBLANES - 2:V7X_SUBLANES - 1, :]
    halo1 = carry_scr[j, V7X_SUBLANES - 1:V7X_SUBLANES, :]
    cu, y = _gated_conv_chunk(xn_scr[...], wu_ref, wb_ref, wc_ref, wz_ref, cw_ref, t, halo0, halo1)
    last_rows = cu[tm - V7X_SUBLANES:tm]
    carry_scr[j] = last_rows
    tail_ref[0] = last_rows
    o_ref[...] += _dot(y.astype(BF16), wo_ref[...])


def _conv_sample_kernel(h_ref, nw_ref, wu_ref, wb_ref, wc_ref, wz_ref, cw_ref, wo_ref, halo0_ref, halo1_ref,
                        o_ref, cu_ref, xn_scr, *, seg_rows):
    j = pl.program_id(0)
    tm, tn = h_ref.shape[0], wu_ref.shape[1]

    @pl.when(j == 0)
    def _():
        h = h_ref[...]
        xn_scr[...] = _rms_norm(h, nw_ref[...]).astype(BF16)
        o_ref[...] = h

    t = _rem(lax.broadcasted_iota(jnp.int32, (tm, tn), 0), seg_rows)
    cu, y = _gated_conv_chunk(xn_scr[...], wu_ref, wb_ref, wc_ref, wz_ref, cw_ref, t,
                              halo0_ref[...], halo1_ref[...])
    cu_ref[...] = cu
    o_ref[...] += _dot(y.astype(BF16), wo_ref[...])


def _w_in_specs(layer, d, tn, col_map):
    n_chunks = d // tn
    return [
        pl.BlockSpec((None, d, tn), functools.partial(col_map, layer, c * n_chunks)) for c in range(4)
    ]


def _conv_layer_prompt(h, norm_w, w_in, conv_w, w_out, layer, seq_len):
    n_rows, d = h.shape
    tm, tn = _row_tile(seq_len, ROW_TILE_CAP), COL_TILE
    tiles_per_seq = seq_len // tm
    n_tiles, n_chunks = n_rows // tm, d // tn
    out, tails = pl.pallas_call(
        functools.partial(_conv_prompt_kernel, tiles_per_seq=tiles_per_seq),
        grid=(n_tiles, n_chunks),
        in_specs=[
            pl.BlockSpec((tm, d), lambda i, j: (i, 0)),
            pl.BlockSpec((1, d), lambda i, j: (0, 0)),
            *_w_in_specs(layer, d, tn, lambda l, c0, i, j: (l, 0, c0 + j)),
            pl.BlockSpec((None, CONV_WIDTH, tn), lambda i, j: (layer, 0, j)),
            pl.BlockSpec((None, tn, d), lambda i, j: (layer, j, 0)),
        ],
        out_specs=[
            pl.BlockSpec((tm, d), lambda i, j: (i, 0)),
            pl.BlockSpec((1, V7X_SUBLANES, tn), lambda i, j: (i, 0, j)),
        ],
        out_shape=[
            jax.ShapeDtypeStruct((n_rows, d), F32),
            jax.ShapeDtypeStruct((n_tiles, V7X_SUBLANES, d), F32),
        ],
        scratch_shapes=[
            pltpu.VMEM((tm, d), BF16),
            pltpu.VMEM((n_chunks, V7X_SUBLANES, tn), F32),
        ],
        compiler_params=_params("arbitrary", "arbitrary"),
        name="conv_mixer_prompt",
    )(h, norm_w, w_in, w_in, w_in, w_in, conv_w, w_out)
    n_seq = n_rows // seq_len
    state = tails.reshape(n_seq, tiles_per_seq, V7X_SUBLANES, d)[:, -1, V7X_SUBLANES - (CONV_WIDTH - 1):]
    return out, state


def _conv_layer_sample(h, norm_w, w_in, conv_w, w_out, layer, state, seq_len):
    n_rows, d = h.shape
    tn = COL_TILE
    n_chunks = d // tn
    n_seq = n_rows // seq_len
    halo0 = jnp.repeat(state[:, 0], seq_len, axis=0)
    halo1 = jnp.repeat(state[:, 1], seq_len, axis=0)
    out, cu = pl.pallas_call(
        functools.partial(_conv_sample_kernel, seg_rows=seq_len),
        grid=(n_chunks,),
        in_specs=[
            pl.BlockSpec((n_rows, d), lambda j: (0, 0)),
            pl.BlockSpec((1, d), lambda j: (0, 0)),
            *_w_in_specs(layer, d, tn, lambda l, c0, j: (l, 0, c0 + j)),
            pl.BlockSpec((None, CONV_WIDTH, tn), lambda j: (layer, 0, j)),
            pl.BlockSpec((None, tn, d), lambda j: (layer, j, 0)),
            pl.BlockSpec((n_rows, tn), lambda j: (0, j)),
            pl.BlockSpec((n_rows, tn), lambda j: (0, j)),
        ],
        out_specs=[
            pl.BlockSpec((n_rows, d), lambda j: (0, 0)),
            pl.BlockSpec((n_rows, tn), lambda j: (0, j)),
        ],
        out_shape=[
            jax.ShapeDtypeStruct((n_rows, d), F32),
            jax.ShapeDtypeStruct((n_rows, d), F32),
        ],
        scratch_shapes=[pltpu.VMEM((n_rows, d), BF16)],
        compiler_params=_params("arbitrary"),
        name="conv_mixer_sample",
    )(h, norm_w, w_in, w_in, w_in, w_in, conv_w, w_out, halo0, halo1)
    new_state = cu.reshape(n_seq, seq_len, d)[:, seq_len - (CONV_WIDTH - 1):]
    return out, new_state


def _attn_inproj_kernel(h_ref, nw_ref, wq_ref, wk_ref, wv_ref, wz_ref, q_ref, k_ref, v_ref, z_ref, xn_scr):
    @pl.when(pl.program_id(1) == 0)
    def _():
        xn_scr[...] = _rms_norm(h_ref[...], nw_ref[...]).astype(BF16)

    xn = xn_scr[...]
    q_ref[...] = _dot(xn, wq_ref[...]).astype(q_ref.dtype)
    k_ref[...] = _dot(xn, wk_ref[...])
    v_ref[...] = _dot(xn, wv_ref[...])
    z_ref[...] = _dot(xn, wz_ref[...]).astype(z_ref.dtype)


def _attn_inproj(h, norm_w, w_in, layer, tm, act_dtype):
    n_rows, d = h.shape
    tn = COL_TILE
    chunk = pl.BlockSpec((tm, tn), lambda i, j: (i, j))
    return pl.pallas_call(
        _attn_inproj_kernel,
        grid=(n_rows // tm, d // tn),
        in_specs=[
            pl.BlockSpec((tm, d), lambda i, j: (i, 0)),
            pl.BlockSpec((1, d), lambda i, j: (0, 0)),
            *_w_in_specs(layer, d, tn, lambda l, c0, i, j: (l, 0, c0 + j)),
        ],
        out_specs=[chunk, chunk, chunk, chunk],
        out_shape=[
            jax.ShapeDtypeStruct((n_rows, d), act_dtype),
            jax.ShapeDtypeStruct((n_rows, d), F32),
            jax.ShapeDtypeStruct((n_rows, d), F32),
            jax.ShapeDtypeStruct((n_rows, d), act_dtype),
        ],
        scratch_shapes=[pltpu.VMEM((tm, d), BF16)],
        compiler_params=_params("arbitrary", "arbitrary"),
        name="attn_inproj",
    )(h, norm_w, w_in, w_in, w_in, w_in)


def _prompt_attn_kernel(rb_ref, q_ref, k_ref, v_ref, z_ref, near_ref, lam_ref, sw_ref,
                        y_ref, kb_scr, vb_scr, *, lam_init):
    head = pl.program_id(1)
    seq_len = q_ref.shape[1]
    padded_len = kb_scr.shape[0]
    tq = near_ref.shape[1]
    kb_scr[0:seq_len] = k_ref[0].astype(BF16)
    vb_scr[0:seq_len] = v_ref[0].astype(BF16)
    if padded_len > seq_len:
        kb_scr[seq_len:padded_len] = jnp.zeros((padded_len - seq_len, HEAD_WIDTH), BF16)
        vb_scr[seq_len:padded_len] = jnp.zeros((padded_len - seq_len, HEAD_WIDTH), BF16)
    lam = _lambda(lam_ref[...], lam_init)
    far_bias = rb_ref[N_BUCKETS - 1, head]
    scale = HEAD_DIM ** -0.5

    for r0 in range(0, seq_len, tq):
        rows = min(tq, seq_len - r0)
        near_lo = max(r0 - MAX_DISTANCE, 0)
        near_hi = r0 + tq
        near_w = near_hi - near_lo
        col0 = near_lo - (r0 - MAX_DISTANCE)
        q = q_ref[0, r0:r0 + rows, :]
        row = lax.broadcasted_iota(jnp.int32, (rows, near_w), 0)
        col = lax.broadcasted_iota(jnp.int32, (rows, near_w), 1)
        visible = col + col0 <= row + MAX_DISTANCE
        near_bias = near_ref[0, 0:rows, col0:col0 + near_w]
        e_near, e_far, denom = [], [], []
        for c in range(2):
            lanes = slice(c * HEAD_DIM, (c + 1) * HEAD_DIM)
            s_near = _dot_nt(q[:, lanes], kb_scr[near_lo:near_hi, lanes]) * scale
            s_near = jnp.where(visible, s_near + near_bias, NEG_INF)
            m = jnp.max(s_near, axis=-1, keepdims=True)
            if near_lo > 0:
                s_far = _dot_nt(q[:, lanes], kb_scr[0:near_lo, lanes]) * scale + far_bias
                m = jnp.maximum(m, jnp.max(s_far, axis=-1, keepdims=True))
                e_far.append(jnp.exp(s_far - m))
            e_near.append(jnp.exp(s_near - m))
            total = jnp.sum(e_near[c], axis=-1, keepdims=True)
            if near_lo > 0:
                total = total + jnp.sum(e_far[c], axis=-1, keepdims=True)
            denom.append(total)
        w1 = 1.0 / denom[0]
        w2 = lam / denom[1]
        a_near = e_near[0] * w1 - e_near[1] * w2
        o = _dot(a_near.astype(BF16), vb_scr[near_lo:near_hi, :])
        if near_lo > 0:
            a_far = e_far[0] * w1 - e_far[1] * w2
            o = o + _dot(a_far.astype(BF16), vb_scr[0:near_lo, :])
        z = z_ref[0, r0:r0 + rows, :].astype(F32)
        y_ref[0, r0:r0 + rows, :] = _sub_norm_gate(o, sw_ref[...], lam_init, z).astype(y_ref.dtype)


def _prompt_attention(q, k, v, z, rel_bias, near_table, lam_params, sub_w, lam_init):
    batch, seq_len, d = q.shape
    n_heads = d // HEAD_WIDTH
    tq = near_table.shape[1]
    padded_len = -(-seq_len // tq) * tq
    head_cols = pl.BlockSpec((1, seq_len, HEAD_WIDTH), lambda b, h: (b, 0, h))
    return pl.pallas_call(
        functools.partial(_prompt_attn_kernel, lam_init=lam_init),
        grid=(batch, n_heads),
        in_specs=[
            pl.BlockSpec(memory_space=pltpu.SMEM),
            head_cols, head_cols, head_cols, head_cols,
            pl.BlockSpec((1, tq, tq + MAX_DISTANCE), lambda b, h: (h, 0, 0)),
            pl.BlockSpec((4, HEAD_DIM), lambda b, h: (0, 0)),
            pl.BlockSpec((1, HEAD_WIDTH), lambda b, h: (0, 0)),
        ],
        out_specs=head_cols,
        out_shape=jax.ShapeDtypeStruct((batch, seq_len, d), BF16),
        scratch_shapes=[
            pltpu.VMEM((padded_len, HEAD_WIDTH), BF16),
            pltpu.VMEM((padded_len, HEAD_WIDTH), BF16),
        ],
        compiler_params=_params("arbitrary", "arbitrary"),
        name="prompt_attention",
    )(rel_bias, q, k, v, z, near_table, lam_params, sub_w)


def _decode_attn_kernel(pt_ref, q_ref, kn_ref, vn_ref, z_ref, *rest, n_pg, lam_init):
    k_pages, v_pages = rest[:n_pg], rest[n_pg:2 * n_pg]
    new_ref, tail_ref, far_ref, lam_ref, sw_ref, y_ref, m_scr, l_scr, acc_scr = rest[2 * n_pg:]
    del pt_ref
    step, n_steps = pl.program_id(1), pl.num_programs(1)
    ls, d = z_ref.shape
    n_heads = d // HEAD_WIDTH
    n_q = n_heads * ls
    page_rows = PAGE_SIZE * n_heads
    scale = HEAD_DIM ** -0.5
    q = [q_ref[0, c].astype(BF16) for c in range(2)]

    def attend(k_blocks, v_blocks, adjust):
        probs, alphas = [], []
        for c in range(2):
            lanes = slice(c * HEAD_DIM, (c + 1) * HEAD_DIM)
            s = jnp.concatenate(
                [adjust(r, _dot_nt(q[c], kb[:, lanes]) * scale) for r, kb in enumerate(k_blocks)], axis=-1)
            m_old = m_scr[c]
            m_new = jnp.maximum(m_old, jnp.max(s, axis=-1, keepdims=True))
            alpha = jnp.exp(m_old - m_new)
            p = jnp.exp(s - m_new)
            l_scr[c] = alpha * l_scr[c] + jnp.sum(p, axis=-1, keepdims=True)
            m_scr[c] = m_new
            probs.append(p)
            alphas.append(alpha)
        p_both = jnp.concatenate(probs, axis=0).astype(BF16)
        pv, lo = None, 0
        for vb in v_blocks:
            part = _dot(p_both[:, lo:lo + vb.shape[0]], vb)
            pv = part if pv is None else pv + part
            lo += vb.shape[0]
        for c in range(2):
            acc_scr[c] = alphas[c] * acc_scr[c] + pv[c * n_q:(c + 1) * n_q]

    @pl.when(step == 0)
    def _():
        m_scr[...] = jnp.full(m_scr.shape, NEG_INF, F32)
        l_scr[...] = jnp.zeros(l_scr.shape, F32)
        acc_scr[...] = jnp.zeros(acc_scr.shape, F32)
        shape = new_ref.shape
        query = _rem(lax.broadcasted_iota(jnp.int32, shape, 0), ls)
        key = _div(lax.broadcasted_iota(jnp.int32, shape, 1), n_heads)
        causal = key <= query

        def adjust_new(r, s):
            return jnp.where(causal, s + new_ref[...], NEG_INF)

        attend([kn_ref[0].astype(BF16)], [vn_ref[0].astype(BF16)], adjust_new)

    is_last = step == n_steps - 1

    def adjust_past(r, s):
        if r == n_pg - 1:
            return s + jnp.where(is_last, tail_ref[...], far_ref[...])
        return s + far_ref[...]

    attend([ref[0, 0].reshape(page_rows, HEAD_WIDTH).astype(BF16) for ref in k_pages],
           [ref[0, 0].reshape(page_rows, HEAD_WIDTH).astype(BF16) for ref in v_pages], adjust_past)

    @pl.when(is_last)
    def _():
        lam = _lambda(lam_ref[...], lam_init)
        o = acc_scr[0] / l_scr[0] - lam * (acc_scr[1] / l_scr[1])
        o = o * lax.rsqrt(jnp.mean(o * o, axis=-1, keepdims=True) + EPS)
        o = o * sw_ref[...] * (1.0 - lam_init)
        for h in range(n_heads):
            cols = slice(h * HEAD_WIDTH, (h + 1) * HEAD_WIDTH)
            y_ref[:, cols] = o[h * ls:(h + 1) * ls] * _silu(z_ref[:, cols])


def _decode_attention(q, k_new, v_new, z, cache_k, cache_v, page_table, layer, new_table, tail_table, far_table,
                      lam_params, sub_w, lam_init, ls):
    n_rows, d = q.shape
    batch, n_pages = page_table.shape
    n_heads = d // HEAD_WIDTH
    n_q = n_heads * ls
    new_rows = new_table.shape[1]
    n_pg = math.gcd(n_pages, PAGES_PER_STEP)
    q_rows = q.reshape(batch, ls, n_heads, 2, HEAD_DIM).transpose(0, 3, 2, 1, 4).reshape(batch, 2, n_q, HEAD_DIM)
    pad = ((0, 0), (0, new_rows - n_q), (0, 0))
    k_rows = jnp.pad(k_new.reshape(batch, n_q, HEAD_WIDTH), pad)
    v_rows = jnp.pad(v_new.reshape(batch, n_q, HEAD_WIDTH), pad)
    page_specs = [
        pl.BlockSpec((1, 1, PAGE_SIZE, n_heads, HEAD_WIDTH), functools.partial(
            lambda r, b, s, pt: (layer, pt[b, s * n_pg + r], 0, 0, 0), r))
        for r in range(n_pg)
    ]
    new_page = pl.BlockSpec((1, new_rows, HEAD_WIDTH), lambda b, s, pt: (b, 0, 0))
    whole = lambda shape: pl.BlockSpec(shape, lambda b, s, pt: (0,) * len(shape))
    grid_spec = pltpu.PrefetchScalarGridSpec(
        num_scalar_prefetch=1,
        grid=(batch, n_pages // n_pg),
        in_specs=[
            pl.BlockSpec((1, 2, n_q, HEAD_DIM), lambda b, s, pt: (b, 0, 0, 0)),
            new_page, new_page,
            pl.BlockSpec((ls, d), lambda b, s, pt: (b, 0)),
            *page_specs, *page_specs,
            whole(new_table.shape), whole(tail_table.shape), whole(far_table.shape),
            whole((4, HEAD_DIM)), whole((1, HEAD_WIDTH)),
        ],
        out_specs=pl.BlockSpec((ls, d), lambda b, s, pt: (b, 0)),
        scratch_shapes=[
            pltpu.VMEM((2, n_q, 1), F32),
            pltpu.VMEM((2, n_q, 1), F32),
            pltpu.VMEM((2, n_q, HEAD_WIDTH), F32),
        ],
    )
    return pl.pallas_call(
        functools.partial(_decode_attn_kernel, n_pg=n_pg, lam_init=lam_init),
        grid_spec=grid_spec,
        out_shape=jax.ShapeDtypeStruct((n_rows, d), F32),
        compiler_params=_params("arbitrary", "arbitrary"),
        name="decode_attention",
    )(page_table, q_rows, k_rows, v_rows, z, *([cache_k] * n_pg), *([cache_v] * n_pg),
      new_table, tail_table, far_table, lam_params, sub_w)


def _outproj_kernel(y_ref, w_ref, h_ref, *rest, final_norm):
    h = h_ref[...] + _dot(y_ref[...].astype(BF16), w_ref[...])
    if final_norm:
        fw_ref, o_ref = rest
        o_ref[...] = _rms_norm(h, fw_ref[...])
    else:
        (o_ref,) = rest
        o_ref[...] = h


def _outproj(y, w_out, layer, h, tm, final_w=None):
    n_rows, d = h.shape
    in_specs = [
        pl.BlockSpec((tm, d), lambda i: (i, 0)),
        pl.BlockSpec((None, d, d), lambda i: (layer, 0, 0)),
        pl.BlockSpec((tm, d), lambda i: (i, 0)),
    ]
    args = [y, w_out, h]
    if final_w is not None:
        in_specs.append(pl.BlockSpec((1, d), lambda i: (0, 0)))
        args.append(final_w)
    return pl.pallas_call(
        functools.partial(_outproj_kernel, final_norm=final_w is not None),
        grid=(n_rows // tm,),
        in_specs=in_specs,
        out_specs=pl.BlockSpec((tm, d), lambda i: (i, 0)),
        out_shape=jax.ShapeDtypeStruct((n_rows, d), F32),
        compiler_params=_params("arbitrary"),
        name="attn_outproj",
    )(*args)


def kernel(x_prompt, x_sample, state_conv, cache_k, cache_v, page_table, meta_tokens, rel_bias, norm_w,
           final_norm_w, conv_w_in, conv_w, conv_w_out, attn_w_in, attn_lambda, attn_subln_w, attn_w_out):
    b, seq, d = x_prompt.shape
    db, ls, _ = x_sample.shape
    depth = norm_w.shape[0]
    lp = N_META + seq
    n_heads = d // HEAD_WIDTH
    assert d % HEAD_WIDTH == 0 and d % COL_TILE == 0
    assert ls >= CONV_WIDTH - 1 and (db * ls) % V7X_BF16_SUBLANES == 0
    assert cache_k.shape[2:] == (PAGE_SIZE, n_heads, HEAD_WIDTH) and cache_v.shape == cache_k.shape
    assert ls <= PAGE_SIZE and V7X_LANES % n_heads == 0
    assert depth % 2 == 0, "the final norm is fused into the last attention mixer's out-projection"

    conv_w_in_b, conv_w_out_b = conv_w_in.astype(BF16), conv_w_out.astype(BF16)
    attn_w_in_b, attn_w_out_b = attn_w_in.astype(BF16), attn_w_out.astype(BF16)

    meta = jnp.broadcast_to(meta_tokens.astype(x_prompt.dtype)[None], (b, N_META, d))
    hp = jnp.concatenate([meta, x_prompt], axis=1).reshape(b * lp, d)
    hs = x_sample.reshape(db * ls, d)
    tm_p = _row_tile(lp, ROW_TILE_CAP)
    tm_s = db * ls

    near_table, new_table, tail_table, far_table = _bias_tables(rel_bias, Q_TILE, ls)

    conv_p, conv_s, k_p, v_p, k_s, v_s = [], [], [], [], [], []
    for i in range(depth):
        nw = norm_w[i].reshape(1, d)
        j = i // 2
        fw = final_norm_w.reshape(1, d) if i == depth - 1 else None
        if i % 2 == 0:
            hp, st_p = _conv_layer_prompt(hp, nw, conv_w_in_b, conv_w, conv_w_out_b, j, lp)
            hs, st_s = _conv_layer_sample(hs, nw, conv_w_in_b, conv_w, conv_w_out_b, j, state_conv[j], ls)
            conv_p.append(st_p)
            conv_s.append(st_s)
        else:
            lam_init = 0.8 - 0.6 * math.exp(-0.3 * i)
            sub_w = attn_subln_w[j].reshape(1, HEAD_WIDTH)
            q, k, v, z = _attn_inproj(hp, nw, attn_w_in_b, j, tm_p, BF16)
            yp = _prompt_attention(q.reshape(b, lp, d), k.reshape(b, lp, d), v.reshape(b, lp, d),
                                   z.reshape(b, lp, d), rel_bias, near_table, attn_lambda[j], sub_w, lam_init)
            hp = _outproj(yp.reshape(b * lp, d), attn_w_out_b, j, hp, tm_p, fw)
            qs, ks, vs, zs = _attn_inproj(hs, nw, attn_w_in_b, j, tm_s, F32)
            ys = _decode_attention(qs, ks, vs, zs, cache_k, cache_v, page_table, j, new_table, tail_table,
                                   far_table, attn_lambda[j], sub_w, lam_init, ls)
            hs = _outproj(ys, attn_w_out_b, j, hs, tm_s, fw)
            k_p.append(k.reshape(b, lp, n_heads, HEAD_WIDTH))
            v_p.append(v.reshape(b, lp, n_heads, HEAD_WIDTH))
            k_s.append(ks.reshape(db, ls, n_heads, HEAD_WIDTH))
            v_s.append(vs.reshape(db, ls, n_heads, HEAD_WIDTH))

    y_prompt = hp.reshape(b, lp, d)[:, N_META:]
    y_sample = hs.reshape(db, ls, d)
    return (y_prompt, y_sample, jnp.stack(conv_p), jnp.stack(conv_s),
            jnp.stack(k_p), jnp.stack(v_p), jnp.stack(k_s), jnp.stack(v_s))
```

```python
import functools
import math

import jax
import jax.numpy as jnp
from jax import lax
from jax.experimental import pallas as pl
from jax.experimental.pallas import tpu as pltpu

N_META = 16
HEAD_DIM = 128
HEAD_WIDTH = 2 * HEAD_DIM
PAGE_SIZE = 128
N_BUCKETS = 32
MAX_EXACT = N_BUCKETS // 2
MAX_DISTANCE = 128
CONV_WIDTH = 3
EPS = 1e-6
NEG_INF = -1e30
LOG2E = math.log2(math.e)
SCORE_SCALE = HEAD_DIM ** -0.5 * LOG2E

BF16 = jnp.bfloat16
F32 = jnp.float32

V7X_LANES = 128
V7X_SUBLANES = 8
V7X_BF16_SUBLANES = 16
V7X_VMEM_LIMIT_BYTES = 56 * 1024 * 1024

ROW_TILE_CAP = 768
COL_TILE = 256
OUT_COL_TILE = 512
Q_TILE = 512
PAGES_PER_STEP = 4


def _row_tile(n_rows, cap):
    best = None
    for t in range(V7X_BF16_SUBLANES, min(n_rows, cap) + 1, V7X_BF16_SUBLANES):
        if n_rows % t == 0:
            best = t
    return n_rows if best is None else best


def _params(*semantics):
    return pltpu.CompilerParams(dimension_semantics=semantics, vmem_limit_bytes=V7X_VMEM_LIMIT_BYTES)


def _dot(a, b):
    return jnp.dot(a, b, preferred_element_type=F32)


def _dot_nt(a, b):
    return lax.dot_general(a, b, (((1,), (1,)), ((), ())), preferred_element_type=F32)


def _rms_norm(x, w):
    return x * lax.rsqrt(jnp.mean(x * x, axis=-1, keepdims=True) + EPS) * w


def _silu(z):
    return z * (1.0 / (1.0 + jnp.exp(-z)))


def _lambda(lp, lam_init):
    a = jnp.sum(lp[0:1] * lp[1:2], axis=-1, keepdims=True)
    b = jnp.sum(lp[2:3] * lp[3:4], axis=-1, keepdims=True)
    return jnp.exp(a) - jnp.exp(b) + lam_init


def _sub_norm_gate(o, sub_w, lam_init, z):
    o = o * lax.rsqrt(jnp.mean(o * o, axis=-1, keepdims=True) + EPS)
    o = o * sub_w * (1.0 - lam_init)
    return o * _silu(z)


def _div(x, n):
    return x >> (n.bit_length() - 1) if n & (n - 1) == 0 else lax.div(x, jnp.int32(n))


def _rem(x, n):
    return x & (n - 1) if n & (n - 1) == 0 else lax.rem(x, jnp.int32(n))


def _to_bf16(src_refs, dst_refs):
    out = []
    for src, dst in zip(src_refs, dst_refs):
        w = src[...].astype(BF16)
        dst[...] = w
        out.append(w)
    return out


def _whole(shape, n_grid_axes):
    return pl.BlockSpec(shape, lambda *_: (0,) * len(shape))


def _bias_tables_kernel(rb_ref, rbl_ref, near_ref, new_ref, tail_ref, far_ref, *, n_heads, ls):
    head = pl.program_id(0)

    def bucket_of(n):
        n = jnp.maximum(n, 0)
        nf = jnp.maximum(n, 1).astype(F32)
        large = MAX_EXACT + (
            jnp.log(nf / MAX_EXACT) / math.log(MAX_DISTANCE / MAX_EXACT) * (N_BUCKETS - MAX_EXACT)
        ).astype(jnp.int32)
        large = jnp.minimum(large, N_BUCKETS - 1)
        return jnp.where(n < MAX_EXACT, n, large)

    shape = near_ref.shape[1:]
    bucket = bucket_of(lax.broadcasted_iota(jnp.int32, shape, 0) + MAX_DISTANCE
                       - lax.broadcasted_iota(jnp.int32, shape, 1))
    near = jnp.zeros(shape, F32)
    for b in range(N_BUCKETS):
        near = jnp.where(bucket == b, rb_ref[b, head], near)
    near_ref[0] = near * LOG2E

    @pl.when(head == 0)
    def _():
        n_q = n_heads * ls

        def slots(shape):
            row = lax.broadcasted_iota(jnp.int32, shape, 0)
            lane = lax.broadcasted_iota(jnp.int32, shape, 1)
            valid = jnp.logical_and(_rem(row, n_heads) == _div(_rem(lane, n_q), ls), lane < 2 * n_q)
            return _div(row, n_heads), _rem(lane, ls), valid

        def paged(ref, offset):
            key, query, valid = slots(ref.shape)
            bucket = bucket_of(query + offset - key)
            bias = jnp.zeros(ref.shape, F32)
            for b in range(N_BUCKETS):
                bias = jnp.where(bucket == b, rbl_ref[b:b + 1, :], bias)
            ref[...] = jnp.where(valid, bias * LOG2E, NEG_INF)

        paged(new_ref, 0)
        paged(tail_ref, PAGE_SIZE)
        tile = (V7X_SUBLANES, V7X_LANES)
        _, _, valid = slots(tile)
        far = jnp.broadcast_to(rbl_ref[N_BUCKETS - 1:N_BUCKETS, :] * LOG2E, tile)
        far_ref[0:8] = jnp.where(valid, far, NEG_INF)
        far_ref[8:16] = jnp.where(valid, far, 0.0)
        far_ref[16:24] = jnp.where(valid, NEG_INF, 0.0)
        far_ref[24:32] = jnp.where(valid, 1.0, 0.0)


def _bias_tables(rel_bias, tq, ls):
    n_heads = rel_bias.shape[1]
    n_q = n_heads * ls
    assert n_heads == V7X_SUBLANES and 2 * n_q <= V7X_LANES
    near_w = tq + MAX_DISTANCE
    page_rows = PAGE_SIZE * n_heads
    new_rows = -(-n_q // V7X_LANES) * V7X_LANES
    lane_head = (jnp.arange(V7X_LANES) % n_q) // ls
    rb_lanes = jnp.where(jnp.arange(V7X_LANES) < 2 * n_q, rel_bias[:, lane_head], 0.0)
    return pl.pallas_call(
        functools.partial(_bias_tables_kernel, n_heads=n_heads, ls=ls),
        grid=(n_heads,),
        in_specs=[pl.BlockSpec(memory_space=pltpu.SMEM), _whole((N_BUCKETS, V7X_LANES), 1)],
        out_specs=[
            pl.BlockSpec((1, tq, near_w), lambda h: (h, 0, 0)),
            _whole((new_rows, V7X_LANES), 1), _whole((page_rows, V7X_LANES), 1),
            _whole((4 * V7X_SUBLANES, V7X_LANES), 1),
        ],
        out_shape=[
            jax.ShapeDtypeStruct((n_heads, tq, near_w), F32),
            jax.ShapeDtypeStruct((new_rows, V7X_LANES), F32),
            jax.ShapeDtypeStruct((page_rows, V7X_LANES), F32),
            jax.ShapeDtypeStruct((4 * V7X_SUBLANES, V7X_LANES), F32),
        ],
        compiler_params=_params("arbitrary"),
        name="bias_tables",
    )(rel_bias, rb_lanes)


def _gated_conv_chunk(xn, wu, wb, wc, wz, cw, t, halo0, halo1):
    u = _dot(xn, wu)
    gate_b = _dot(xn, wb)
    gate_c = _dot(xn, wc)
    z = _dot(xn, wz)
    cu = gate_c * u
    prev1 = jnp.where(t >= 1, pltpu.roll(cu, 1, 0), halo1)
    prev2 = jnp.where(t >= 2, pltpu.roll(cu, 2, 0), jnp.where(t == 1, halo1, halo0))
    conv = prev2 * cw[0:1] + prev1 * cw[1:2] + cu * cw[2:3]
    return cu, gate_b * conv * _silu(z)


def _conv_prompt_kernel(*refs, tiles_per_seq, from_input):
    if from_input:
        x_ref, meta_ref, *refs = refs
    else:
        h_ref, *refs = refs
    nw_ref, wu_ref, wb_ref, wc_ref, wz_ref, cw_ref, wo_ref, o_ref, tail_ref, xn_scr, carry_scr = refs
    i, j = pl.program_id(0), pl.program_id(1)
    tm, tn = o_ref.shape[0], wu_ref.shape[1]
    starts_seq = i % tiles_per_seq == 0

    def start_tile(h):
        xn_scr[...] = _rms_norm(h, nw_ref[...]).astype(BF16)
        o_ref[...] = h

    if from_input:
        @pl.when(jnp.logical_and(j == 0, starts_seq))
        def _():
            start_tile(jnp.concatenate([meta_ref[...], x_ref[0:tm - N_META, :]], axis=0))

        @pl.when(jnp.logical_and(j == 0, jnp.logical_not(starts_seq)))
        def _():
            start_tile(x_ref[...])
    else:
        @pl.when(j == 0)
        def _():
            start_tile(h_ref[...])

    @pl.when(jnp.logical_and(j == 0, starts_seq))
    def _():
        carry_scr[...] = jnp.zeros(carry_scr.shape, F32)

    t = lax.broadcasted_iota(jnp.int32, (tm, tn), 0)
    halo0 = carry_scr[j, V7X_SUBLANES - 2:V7X_SUBLANES - 1, :]
    halo1 = carry_scr[j, V7X_SUBLANES - 1:V7X_SUBLANES, :]
    cu, y = _gated_conv_chunk(xn_scr[...], wu_ref[...], wb_ref[...], wc_ref[...], wz_ref[...], cw_ref[...],
                              t, halo0, halo1)
    last_rows = cu[tm - V7X_SUBLANES:tm]
    carry_scr[j] = last_rows
    tail_ref[0] = last_rows
    o_ref[...] += _dot(y.astype(BF16), wo_ref[...])


def _conv_sample_kernel(h_ref, nw_ref, wu_ref, wb_ref, wc_ref, wz_ref, cw_ref, wo_ref, halo0_ref, halo1_ref,
                        o_ref, cu_ref, wu_out, wb_out, wc_out, wz_out, wo_out, xn_scr, *, seg_rows):
    j = pl.program_id(0)
    tm, tn = h_ref.shape[0], wu_ref.shape[1]

    @pl.when(j == 0)
    def _():
        h = h_ref[...]
        xn_scr[...] = _rms_norm(h, nw_ref[...]).astype(BF16)
        o_ref[...] = h

    wu, wb, wc, wz, wo = _to_bf16((wu_ref, wb_ref, wc_ref, wz_ref, wo_ref), (wu_out, wb_out, wc_out, wz_out, wo_out))
    t = _rem(lax.broadcasted_iota(jnp.int32, (tm, tn), 0), seg_rows)
    cu, y = _gated_conv_chunk(xn_scr[...], wu, wb, wc, wz, cw_ref[...], t, halo0_ref[...], halo1_ref[...])
    cu_ref[...] = cu
    o_ref[...] += _dot(y.astype(BF16), wo)


def _w_in_specs(layer, d, tn, col_map):
    n_chunks = d // tn
    return [
        pl.BlockSpec((None, d, tn), functools.partial(col_map, layer, c * n_chunks)) for c in range(4)
    ]


def _conv_layer_prompt(h, norm_w, w_in_groups, conv_w, w_out, layer, batch, seq_len):
    from_input = isinstance(h, tuple)
    d = w_out.shape[0]
    n_rows = batch * seq_len
    tm, tn = _row_tile(seq_len, ROW_TILE_CAP), COL_TILE
    tiles_per_seq = seq_len // tm
    n_tiles, n_chunks = n_rows // tm, d // tn
    if from_input:
        x_len = seq_len - N_META
        assert tm > N_META and tiles_per_seq * tm - N_META == x_len

        assert x_len % V7X_BF16_SUBLANES == 0 and N_META % V7X_BF16_SUBLANES == 0

        def x_window(i, j):
            row = (i // tiles_per_seq) * x_len + jnp.maximum((i % tiles_per_seq) * tm - N_META, 0)
            return pl.multiple_of(row, V7X_BF16_SUBLANES), 0

        h_specs = [pl.BlockSpec((pl.Element(tm), pl.Element(d)), x_window), _whole((N_META, d), 2)]
        h_args = list(h)
    else:
        h_specs = [pl.BlockSpec((tm, d), lambda i, j: (i, 0))]
        h_args = [h]
    w_chunk = pl.BlockSpec((d, tn), lambda i, j: (0, j))
    out, tails = pl.pallas_call(
        functools.partial(_conv_prompt_kernel, tiles_per_seq=tiles_per_seq, from_input=from_input),
        grid=(n_tiles, n_chunks),
        in_specs=[
            *h_specs,
            _whole((1, d), 2),
            w_chunk, w_chunk, w_chunk, w_chunk,
            pl.BlockSpec((None, CONV_WIDTH, tn), lambda i, j: (layer, 0, j)),
            pl.BlockSpec((tn, d), lambda i, j: (j, 0)),
        ],
        out_specs=[
            pl.BlockSpec((tm, d), lambda i, j: (i, 0)),
            pl.BlockSpec((1, V7X_SUBLANES, tn), lambda i, j: (i, 0, j)),
        ],
        out_shape=[
            jax.ShapeDtypeStruct((n_rows, d), F32),
            jax.ShapeDtypeStruct((n_tiles, V7X_SUBLANES, d), F32),
        ],
        scratch_shapes=[
            pltpu.VMEM((tm, d), BF16),
            pltpu.VMEM((n_chunks, V7X_SUBLANES, tn), F32),
        ],
        compiler_params=_params("arbitrary", "arbitrary"),
        name="conv_mixer_prompt",
    )(*h_args, norm_w, *w_in_groups, conv_w, w_out)
    state = tails.reshape(batch, tiles_per_seq, V7X_SUBLANES, d)[:, -1, V7X_SUBLANES - (CONV_WIDTH - 1):]
    return out, state


def _conv_layer_sample(h, norm_w, w_in, conv_w, w_out, layer, state, seq_len):
    n_rows, d = h.shape
    tn = COL_TILE
    n_chunks = d // tn
    n_seq = n_rows // seq_len
    halo0 = jnp.repeat(state[:, 0], seq_len, axis=0)
    halo1 = jnp.repeat(state[:, 1], seq_len, axis=0)
    rows_chunk = pl.BlockSpec((n_rows, tn), lambda j: (0, j))
    w_chunk = pl.BlockSpec((d, tn), lambda j: (0, j))
    w_bf16 = jax.ShapeDtypeStruct((d, d), BF16)
    out, cu, *weights = pl.pallas_call(
        functools.partial(_conv_sample_kernel, seg_rows=seq_len),
        grid=(n_chunks,),
        in_specs=[
            _whole((n_rows, d), 1),
            _whole((1, d), 1),
            *_w_in_specs(layer, d, tn, lambda l, c0, j: (l, 0, c0 + j)),
            pl.BlockSpec((None, CONV_WIDTH, tn), lambda j: (layer, 0, j)),
            pl.BlockSpec((None, tn, d), lambda j: (layer, j, 0)),
            rows_chunk, rows_chunk,
        ],
        out_specs=[
            _whole((n_rows, d), 1), rows_chunk,
            w_chunk, w_chunk, w_chunk, w_chunk,
            pl.BlockSpec((tn, d), lambda j: (j, 0)),
        ],
        out_shape=[
            jax.ShapeDtypeStruct((n_rows, d), F32),
            jax.ShapeDtypeStruct((n_rows, d), F32),
            w_bf16, w_bf16, w_bf16, w_bf16, w_bf16,
        ],
        scratch_shapes=[pltpu.VMEM((n_rows, d), BF16)],
        compiler_params=_params("arbitrary"),
        name="conv_mixer_sample",
    )(h, norm_w, w_in, w_in, w_in, w_in, conv_w, w_out, halo0, halo1)
    new_state = cu.reshape(n_seq, seq_len, d)[:, seq_len - (CONV_WIDTH - 1):]
    return out, new_state, weights[:4], weights[4]


def _attn_inproj_kernel(h_ref, nw_ref, wq_ref, wk_ref, wv_ref, wz_ref, q_ref, k_ref, v_ref, z_ref, *rest):
    *w_outs, xn_scr = rest

    @pl.when(pl.program_id(1) == 0)
    def _():
        xn_scr[...] = _rms_norm(h_ref[...], nw_ref[...]).astype(BF16)

    w_refs = (wq_ref, wk_ref, wv_ref, wz_ref)
    wq, wk, wv, wz = _to_bf16(w_refs, w_outs) if w_outs else [w[...] for w in w_refs]
    xn = xn_scr[...]
    q_ref[...] = (_dot(xn, wq) * SCORE_SCALE).astype(q_ref.dtype)
    k_ref[...] = _dot(xn, wk)
    v_ref[...] = _dot(xn, wv)
    z_ref[...] = _dot(xn, wz).astype(z_ref.dtype)


def _attn_inproj(h, norm_w, weights, tm, act_dtype, layer=None):
    n_rows, d = h.shape
    tn = COL_TILE
    chunk = pl.BlockSpec((tm, tn), lambda i, j: (i, j))
    w_chunk = pl.BlockSpec((d, tn), lambda i, j: (0, j))
    emit = layer is not None
    if emit:
        w_specs = _w_in_specs(layer, d, tn, lambda l, c0, i, j: (l, 0, c0 + j))
        w_args = [weights] * 4
    else:
        w_specs = [w_chunk] * 4
        w_args = list(weights)
    outs = pl.pallas_call(
        _attn_inproj_kernel,
        grid=(n_rows // tm, d // tn),
        in_specs=[pl.BlockSpec((tm, d), lambda i, j: (i, 0)), _whole((1, d), 2), *w_specs],
        out_specs=[chunk, chunk, chunk, chunk] + ([w_chunk] * 4 if emit else []),
        out_shape=[
            jax.ShapeDtypeStruct((n_rows, d), act_dtype),
            jax.ShapeDtypeStruct((n_rows, d), F32),
            jax.ShapeDtypeStruct((n_rows, d), F32),
            jax.ShapeDtypeStruct((n_rows, d), act_dtype),
        ] + ([jax.ShapeDtypeStruct((d, d), BF16)] * 4 if emit else []),
        scratch_shapes=[pltpu.VMEM((tm, d), BF16)],
        compiler_params=_params("arbitrary", "arbitrary"),
        name="attn_inproj",
    )(h, norm_w, *w_args)
    return (outs[:4], outs[4:]) if emit else outs


def _prompt_attn_kernel(rb_ref, q_ref, k_ref, v_ref, z_ref, near_ref, lam_ref, sw_ref,
                        y_ref, kb_scr, vb_scr, *, lam_init):
    head = pl.program_id(1)
    seq_len = q_ref.shape[1]
    padded_len = kb_scr.shape[0]
    tq = near_ref.shape[1]
    kb_scr[0:seq_len] = k_ref[0].astype(BF16)
    vb_scr[0:seq_len] = v_ref[0].astype(BF16)
    if padded_len > seq_len:
        kb_scr[seq_len:padded_len] = jnp.zeros((padded_len - seq_len, HEAD_WIDTH), BF16)
        vb_scr[seq_len:padded_len] = jnp.zeros((padded_len - seq_len, HEAD_WIDTH), BF16)
    lam = _lambda(lam_ref[...], lam_init)
    far_bias = rb_ref[N_BUCKETS - 1, head] * LOG2E

    for r0 in range(0, seq_len, tq):
        rows = min(tq, seq_len - r0)
        near_lo = max(r0 - MAX_DISTANCE, 0)
        near_hi = r0 + tq
        near_w = near_hi - near_lo
        col0 = near_lo - (r0 - MAX_DISTANCE)
        q = q_ref[0, r0:r0 + rows, :]
        row = lax.broadcasted_iota(jnp.int32, (rows, near_w), 0)
        col = lax.broadcasted_iota(jnp.int32, (rows, near_w), 1)
        visible = col + col0 <= row + MAX_DISTANCE
        near_bias = near_ref[0, 0:rows, col0:col0 + near_w]
        e_near, e_far, denom = [], [], []
        for c in range(2):
            lanes = slice(c * HEAD_DIM, (c + 1) * HEAD_DIM)
            s_near = _dot_nt(q[:, lanes], kb_scr[near_lo:near_hi, lanes])
            s_near = jnp.where(visible, s_near + near_bias, NEG_INF)
            m = jnp.max(s_near, axis=-1, keepdims=True)
            if near_lo > 0:
                s_far = _dot_nt(q[:, lanes], kb_scr[0:near_lo, lanes])
                m = jnp.maximum(m, jnp.max(s_far, axis=-1, keepdims=True) + far_bias)
                e_far.append(jnp.exp2(s_far - (m - far_bias)))
            e_near.append(jnp.exp2(s_near - m))
            total = jnp.sum(e_near[c], axis=-1, keepdims=True)
            if near_lo > 0:
                total = total + jnp.sum(e_far[c], axis=-1, keepdims=True)
            denom.append(total)
        w1 = 1.0 / denom[0]
        w2 = lam / denom[1]
        a_near = e_near[0] * w1 - e_near[1] * w2
        o = _dot(a_near.astype(BF16), vb_scr[near_lo:near_hi, :])
        if near_lo > 0:
            a_far = e_far[0] * w1 - e_far[1] * w2
            o = o + _dot(a_far.astype(BF16), vb_scr[0:near_lo, :])
        z = z_ref[0, r0:r0 + rows, :].astype(F32)
        y_ref[0, r0:r0 + rows, :] = _sub_norm_gate(o, sw_ref[...], lam_init, z).astype(y_ref.dtype)


def _prompt_attention(q, k, v, z, rel_bias, near_table, lam_params, sub_w, lam_init):
    batch, seq_len, d = q.shape
    n_heads = d // HEAD_WIDTH
    tq = near_table.shape[1]
    padded_len = -(-seq_len // tq) * tq
    head_cols = pl.BlockSpec((1, seq_len, HEAD_WIDTH), lambda b, h: (b, 0, h))
    return pl.pallas_call(
        functools.partial(_prompt_attn_kernel, lam_init=lam_init),
        grid=(batch, n_heads),
        in_specs=[
            pl.BlockSpec(memory_space=pltpu.SMEM),
            head_cols, head_cols, head_cols, head_cols,
            pl.BlockSpec((1, tq, tq + MAX_DISTANCE), lambda b, h: (h, 0, 0)),
            _whole((4, HEAD_DIM), 2),
            _whole((1, HEAD_WIDTH), 2),
        ],
        out_specs=head_cols,
        out_shape=jax.ShapeDtypeStruct((batch, seq_len, d), BF16),
        scratch_shapes=[
            pltpu.VMEM((padded_len, HEAD_WIDTH), BF16),
            pltpu.VMEM((padded_len, HEAD_WIDTH), BF16),
        ],
        compiler_params=_params("arbitrary", "arbitrary"),
        name="prompt_attention",
    )(rel_bias, q, k, v, z, near_table, lam_params, sub_w)


def _decode_attn_kernel(pt_ref, q_ref, kn_ref, vn_ref, z_ref, *rest, n_pg, lam_init):
    k_pages, v_pages = rest[:n_pg], rest[n_pg:2 * n_pg]
    (new_ref, tail_ref, far_ref, lam_ref, sw_ref, y_ref,
     s_scr, m_seen_scr, m_used_scr, l_scr, acc_scr) = rest[2 * n_pg:]
    del pt_ref
    step, n_groups = pl.program_id(1), pl.num_programs(1) - 1
    ls, d = z_ref.shape
    n_heads = d // HEAD_WIDTH
    n_q = n_heads * ls
    page_rows = PAGE_SIZE * n_heads
    tile = (V7X_SUBLANES, V7X_LANES)
    q_cols = q_ref[0]
    far_bias, far_bias_or_0, m_start, valid = far_ref[0:8], far_ref[8:16], far_ref[16:24], far_ref[24:32]
    slot = lax.rem(step, 2)

    def per_tile(x):
        return x.reshape(x.shape[0] // V7X_SUBLANES, *tile)

    def lanes_to_rows(stat):
        per_lane = jnp.sum(stat * valid, axis=0, keepdims=True)
        return jnp.broadcast_to(per_lane, tile).T[:, 0:1]

    def fold(probs, values, m_from, m_to):
        alpha = jnp.exp2(m_from - m_to)
        total = alpha * l_scr[...]
        pv = None
        for p, vb in zip(probs, values):
            total = total + jnp.sum(p, axis=0)
            part = lax.dot_general(p.reshape(vb.shape[0], V7X_LANES), vb, (((0,), (0,)), ((), ())),
                                   preferred_element_type=F32)
            pv = part if pv is None else pv + part
        l_scr[...] = total
        acc_scr[...] = lanes_to_rows(alpha) * acc_scr[...] + pv

    @pl.when(step == 0)
    def _():
        m_seen_scr[...] = m_start
        m_used_scr[...] = m_start
        l_scr[...] = jnp.zeros(l_scr.shape, F32)
        acc_scr[...] = jnp.zeros(acc_scr.shape, F32)
        s_scr[1] = jnp.full(s_scr.shape[1:], NEG_INF, F32)

    keys_are_tail = step >= n_groups - 1
    values_are_tail = step == n_groups

    m_use = m_seen_scr[...]
    prev = s_scr.at[1 - slot]
    probs = []
    for r in range(n_pg):
        s = per_tile(prev[r * page_rows:(r + 1) * page_rows])
        if r == n_pg - 1:
            probs.append(jnp.exp2(s + jnp.where(values_are_tail, per_tile(tail_ref[...]), far_bias) - m_use))
        else:
            probs.append(jnp.exp2(s + (far_bias - m_use)))
    fold(probs, [ref[0, 0].reshape(page_rows, HEAD_WIDTH) for ref in v_pages], m_used_scr[...], m_use)
    m_used_scr[...] = m_use

    m_new = m_use
    for r, ref in enumerate(k_pages):
        logits = _dot(ref[0, 0].reshape(page_rows, HEAD_WIDTH), q_cols)
        s_scr[slot, r * page_rows:(r + 1) * page_rows] = logits
        s = per_tile(logits)
        if r == n_pg - 1:
            s = s + jnp.where(keys_are_tail, per_tile(tail_ref[...]), far_bias_or_0)
            m_new = jnp.maximum(m_new, jnp.max(s, axis=0))
        else:
            m_new = jnp.maximum(m_new, jnp.max(s, axis=0) + far_bias_or_0)
    m_seen_scr[...] = m_new

    @pl.when(step == n_groups)
    def _():
        shape = new_ref.shape
        key = _div(lax.broadcasted_iota(jnp.int32, shape, 0), n_heads)
        query = _rem(lax.broadcasted_iota(jnp.int32, shape, 1), ls)
        s = per_tile(_dot(kn_ref[0], q_cols) + jnp.where(key <= query, new_ref[...], NEG_INF))
        m_all = jnp.maximum(m_use, jnp.max(s, axis=0))
        fold([jnp.exp2(s - m_all)], [vn_ref[0]], m_use, m_all)
        lam = _lambda(lam_ref[...], lam_init)
        o = acc_scr[...] / lanes_to_rows(l_scr[...])
        o = o[0:n_q] - lam * o[n_q:2 * n_q]
        o = o * lax.rsqrt(jnp.mean(o * o, axis=-1, keepdims=True) + EPS)
        o = o * sw_ref[...] * (1.0 - lam_init)
        for h in range(n_heads):
            cols = slice(h * HEAD_WIDTH, (h + 1) * HEAD_WIDTH)
            y_ref[:, cols] = o[h * ls:(h + 1) * ls] * _silu(z_ref[:, cols])


def _decode_attention(q, k_new, v_new, z, cache_k, cache_v, page_table, layer, new_table, tail_table, far_table,
                      lam_params, sub_w, lam_init, ls):
    n_rows, d = q.shape
    batch, n_pages = page_table.shape
    n_heads = d // HEAD_WIDTH
    n_q = n_heads * ls
    new_rows = new_table.shape[0]
    n_pg = math.gcd(n_pages, PAGES_PER_STEP)
    n_groups = n_pages // n_pg
    q_t = q.reshape(batch, ls, n_heads, 2, HEAD_DIM).transpose(0, 3, 4, 2, 1).reshape(batch, 2, HEAD_DIM, n_q)
    q_cols = jnp.zeros((batch, HEAD_WIDTH, V7X_LANES), F32)
    q_cols = q_cols.at[:, :HEAD_DIM, :n_q].set(q_t[:, 0]).at[:, HEAD_DIM:, n_q:2 * n_q].set(q_t[:, 1])
    pad = ((0, 0), (0, new_rows - n_q), (0, 0))
    k_rows = jnp.pad(k_new.reshape(batch, n_q, HEAD_WIDTH), pad)
    v_rows = jnp.pad(v_new.reshape(batch, n_q, HEAD_WIDTH), pad)

    def page_specs(group_of_step):
        return [
            pl.BlockSpec((1, 1, PAGE_SIZE, n_heads, HEAD_WIDTH), functools.partial(
                lambda r, b, s, pt: (layer, pt[b, group_of_step(s) * n_pg + r], 0, 0, 0), r))
            for r in range(n_pg)
        ]

    new_page = pl.BlockSpec((1, new_rows, HEAD_WIDTH), lambda b, s, pt: (b, 0, 0))
    grid_spec = pltpu.PrefetchScalarGridSpec(
        num_scalar_prefetch=1,
        grid=(batch, n_groups + 1),
        in_specs=[
            pl.BlockSpec((1, HEAD_WIDTH, V7X_LANES), lambda b, s, pt: (b, 0, 0)),
            new_page, new_page,
            pl.BlockSpec((ls, d), lambda b, s, pt: (b, 0)),
            *page_specs(lambda s: jnp.minimum(s, n_groups - 1)), *page_specs(lambda s: jnp.maximum(s - 1, 0)),
            _whole(new_table.shape, 3), _whole(tail_table.shape, 3), _whole(far_table.shape, 3),
            _whole((4, HEAD_DIM), 3), _whole((1, HEAD_WIDTH), 3),
        ],
        out_specs=pl.BlockSpec((ls, d), lambda b, s, pt: (b, 0)),
        scratch_shapes=[
            pltpu.VMEM((2, n_pg * PAGE_SIZE * n_heads, V7X_LANES), F32),
            pltpu.VMEM((V7X_SUBLANES, V7X_LANES), F32),
            pltpu.VMEM((V7X_SUBLANES, V7X_LANES), F32),
            pltpu.VMEM((V7X_SUBLANES, V7X_LANES), F32),
            pltpu.VMEM((V7X_LANES, HEAD_WIDTH), F32),
        ],
    )
    return pl.pallas_call(
        functools.partial(_decode_attn_kernel, n_pg=n_pg, lam_init=lam_init),
        grid_spec=grid_spec,
        out_shape=jax.ShapeDtypeStruct((n_rows, d), F32),
        compiler_params=_params("arbitrary", "arbitrary"),
        name="decode_attention",
    )(page_table, q_cols, k_rows, v_rows, z, *([cache_k] * n_pg), *([cache_v] * n_pg),
      new_table, tail_table, far_table, lam_params, sub_w)


def _outproj_kernel(y_ref, w_ref, h_ref, *rest, final_norm):
    h = h_ref[...] + _dot(y_ref[...], w_ref[...])
    if final_norm:
        fw_ref, o_ref = rest
        o_ref[...] = _rms_norm(h, fw_ref[...])
    else:
        (o_ref,) = rest
        o_ref[...] = h


def _outproj_prompt(y, w_out, h, batch, seq_len, final_w=None):
    n_rows, d = h.shape
    w_spec = _whole((d, d), 1)
    if final_w is None:
        tm = _row_tile(seq_len, ROW_TILE_CAP)
        rows = pl.BlockSpec((tm, d), lambda i: (i, 0))
        return pl.pallas_call(
            functools.partial(_outproj_kernel, final_norm=False),
            grid=(n_rows // tm,),
            in_specs=[rows, w_spec, rows],
            out_specs=rows,
            out_shape=jax.ShapeDtypeStruct((n_rows, d), F32),
            compiler_params=_params("arbitrary"),
            name="attn_outproj",
        )(y, w_out, h)
    out_len = seq_len - N_META
    tm = _row_tile(out_len, ROW_TILE_CAP)
    tiles_per_seq = out_len // tm
    assert seq_len % V7X_BF16_SUBLANES == 0 and N_META % V7X_BF16_SUBLANES == 0

    def rows_window(i):
        row = (i // tiles_per_seq) * seq_len + N_META + (i % tiles_per_seq) * tm
        return pl.multiple_of(row, V7X_BF16_SUBLANES), 0

    window = pl.BlockSpec((pl.Element(tm), pl.Element(d)), rows_window)
    return pl.pallas_call(
        functools.partial(_outproj_kernel, final_norm=True),
        grid=(batch * tiles_per_seq,),
        in_specs=[window, w_spec, window, _whole((1, d), 1)],
        out_specs=pl.BlockSpec((tm, d), lambda i: (i, 0)),
        out_shape=jax.ShapeDtypeStruct((batch * out_len, d), F32),
        compiler_params=_params("arbitrary"),
        name="attn_outproj_final",
    )(y, w_out, h, final_w)


def _outproj_sample_kernel(y_ref, w_ref, h_ref, o_ref, w_out):
    (w,) = _to_bf16((w_ref,), (w_out,))
    o_ref[...] = h_ref[...] + _dot(y_ref[...].astype(BF16), w)


def _outproj_sample(y, w_out, layer, h):
    n_rows, d = h.shape
    tn = OUT_COL_TILE
    cols = pl.BlockSpec((n_rows, tn), lambda j: (0, j))
    return pl.pallas_call(
        _outproj_sample_kernel,
        grid=(d // tn,),
        in_specs=[_whole((n_rows, d), 1), pl.BlockSpec((None, d, tn), lambda j: (layer, 0, j)), cols],
        out_specs=[cols, pl.BlockSpec((d, tn), lambda j: (0, j))],
        out_shape=[jax.ShapeDtypeStruct((n_rows, d), F32), jax.ShapeDtypeStruct((d, d), BF16)],
        compiler_params=_params("arbitrary"),
        name="attn_outproj_sample",
    )(y, w_out, h)


def _final_norm_kernel(h_ref, w_ref, o_ref):
    o_ref[...] = _rms_norm(h_ref[...], w_ref[...])


def _final_norm(h, w):
    return pl.pallas_call(
        _final_norm_kernel, out_shape=jax.ShapeDtypeStruct(h.shape, F32), name="final_norm_sample")(h, w)


def kernel(x_prompt, x_sample, state_conv, cache_k, cache_v, page_table, meta_tokens, rel_bias, norm_w,
           final_norm_w, conv_w_in, conv_w, conv_w_out, attn_w_in, attn_lambda, attn_subln_w, attn_w_out):
    b, seq, d = x_prompt.shape
    db, ls, _ = x_sample.shape
    depth = norm_w.shape[0]
    lp = N_META + seq
    n_heads = d // HEAD_WIDTH
    assert d % HEAD_WIDTH == 0 and d % COL_TILE == 0 and d % OUT_COL_TILE == 0
    assert ls >= CONV_WIDTH - 1 and (db * ls) % V7X_BF16_SUBLANES == 0
    assert cache_k.shape[2:] == (PAGE_SIZE, n_heads, HEAD_WIDTH) and cache_v.shape == cache_k.shape
    assert ls <= PAGE_SIZE
    assert depth % 2 == 0, "the final norm is fused into the last attention mixer's out-projection"

    hp = (x_prompt.reshape(b * seq, d), meta_tokens.astype(x_prompt.dtype))
    hs = x_sample.reshape(db * ls, d)
    tm_p = _row_tile(lp, ROW_TILE_CAP)
    fw = final_norm_w.reshape(1, d)

    near_table, new_table, tail_table, far_table = _bias_tables(rel_bias, Q_TILE, ls)

    conv_p, conv_s, k_p, v_p, k_s, v_s = [], [], [], [], [], []
    for i in range(depth):
        nw = norm_w[i].reshape(1, d)
        j = i // 2
        last = i == depth - 1
        if i % 2 == 0:
            hs, st_s, w_in_b, w_out_b = _conv_layer_sample(hs, nw, conv_w_in, conv_w, conv_w_out, j, state_conv[j], ls)
            hp, st_p = _conv_layer_prompt(hp, nw, w_in_b, conv_w, w_out_b, j, b, lp)
            conv_p.append(st_p)
            conv_s.append(st_s)
        else:
            lam_init = 0.8 - 0.6 * math.exp(-0.3 * i)
            sub_w = attn_subln_w[j].reshape(1, HEAD_WIDTH)
            (qs, ks, vs, zs), w_in_b = _attn_inproj(hs, nw, attn_w_in, db * ls, F32, layer=j)
            ys = _decode_attention(qs, ks, vs, zs, cache_k, cache_v, page_table, j, new_table, tail_table,
                                   far_table, attn_lambda[j], sub_w, lam_init, ls)
            hs, w_out_b = _outproj_sample(ys, attn_w_out, j, hs)
            q, k, v, z = _attn_inproj(hp, nw, w_in_b, tm_p, BF16)
            yp = _prompt_attention(q.reshape(b, lp, d), k.reshape(b, lp, d), v.reshape(b, lp, d),
                                   z.reshape(b, lp, d), rel_bias, near_table, attn_lambda[j], sub_w, lam_init)
            hp = _outproj_prompt(yp.reshape(b * lp, d), w_out_b, hp, b, lp, fw if last else None)
            k_p.append(k.reshape(b, lp, n_heads, HEAD_WIDTH))
            v_p.append(v.reshape(b, lp, n_heads, HEAD_WIDTH))
            k_s.append(ks.reshape(db, ls, n_heads, HEAD_WIDTH))
            v_s.append(vs.reshape(db, ls, n_heads, HEAD_WIDTH))

    y_prompt = hp.reshape(b, seq, d)
    y_sample = _final_norm(hs, fw).reshape(db, ls, d)
    return (y_prompt, y_sample, jnp.stack(conv_p), jnp.stack(conv_s),
            jnp.stack(k_p), jnp.stack(v_p), jnp.stack(k_s), jnp.stack(v_s))
```

```python
import functools
import math

import jax
import jax.numpy as jnp
from jax import lax
from jax.experimental import pallas as pl
from jax.experimental.pallas import tpu as pltpu

N_META = 16
HEAD_DIM = 128
HEAD_WIDTH = 2 * HEAD_DIM
PAGE_SIZE = 128
N_BUCKETS = 32
MAX_EXACT = N_BUCKETS // 2
MAX_DISTANCE = 128
CONV_WIDTH = 3
EPS = 1e-6
NEG_INF = -1e30
LOG2E = math.log2(math.e)
SCORE_SCALE = HEAD_DIM ** -0.5 * LOG2E

BF16 = jnp.bfloat16
F32 = jnp.float32

V7X_LANES = 128
V7X_SUBLANES = 8
V7X_BF16_SUBLANES = 16
V7X_VMEM_LIMIT_BYTES = 56 * 1024 * 1024

ROW_TILE_CAP = 768
COL_TILE = 256
OUT_COL_TILE = 512
Q_TILE = 512
PAGES_PER_STEP = 4


def _row_tile(n_rows, cap):
    best = None
    for t in range(V7X_BF16_SUBLANES, min(n_rows, cap) + 1, V7X_BF16_SUBLANES):
        if n_rows % t == 0:
            best = t
    return n_rows if best is None else best


def _params(*semantics):
    return pltpu.CompilerParams(dimension_semantics=semantics, vmem_limit_bytes=V7X_VMEM_LIMIT_BYTES)


def _dot(a, b):
    return jnp.dot(a, b, preferred_element_type=F32)


def _dot_nt(a, b):
    return lax.dot_general(a, b, (((1,), (1,)), ((), ())), preferred_element_type=F32)


def _rms_norm(x, w):
    return x * lax.rsqrt(jnp.mean(x * x, axis=-1, keepdims=True) + EPS) * w


def _silu(z):
    return z * (1.0 / (1.0 + jnp.exp(-z)))


def _lambda(lp, lam_init):
    a = jnp.sum(lp[0:1] * lp[1:2], axis=-1, keepdims=True)
    b = jnp.sum(lp[2:3] * lp[3:4], axis=-1, keepdims=True)
    return jnp.exp(a) - jnp.exp(b) + lam_init


def _sub_norm_gate(o, sub_w, lam_init, z):
    o = o * lax.rsqrt(jnp.mean(o * o, axis=-1, keepdims=True) + EPS)
    o = o * sub_w * (1.0 - lam_init)
    return o * _silu(z)


def _div(x, n):
    return x >> (n.bit_length() - 1) if n & (n - 1) == 0 else lax.div(x, jnp.int32(n))


def _rem(x, n):
    return x & (n - 1) if n & (n - 1) == 0 else lax.rem(x, jnp.int32(n))


def _to_bf16(src_refs, dst_refs):
    out = []
    for src, dst in zip(src_refs, dst_refs):
        w = src[...].astype(BF16)
        dst[...] = w
        out.append(w)
    return out


def _whole(shape, n_grid_axes):
    return pl.BlockSpec(shape, lambda *_: (0,) * len(shape))


def _bias_tables_kernel(rb_ref, rbl_ref, near_ref, new_ref, tail_ref, far_ref, *, n_heads, ls):
    head = pl.program_id(0)

    def bucket_of(n):
        n = jnp.maximum(n, 0)
        nf = jnp.maximum(n, 1).astype(F32)
        large = MAX_EXACT + (
            jnp.log(nf / MAX_EXACT) / math.log(MAX_DISTANCE / MAX_EXACT) * (N_BUCKETS - MAX_EXACT)
        ).astype(jnp.int32)
        large = jnp.minimum(large, N_BUCKETS - 1)
        return jnp.where(n < MAX_EXACT, n, large)

    shape = near_ref.shape[1:]
    bucket = bucket_of(lax.broadcasted_iota(jnp.int32, shape, 0) + MAX_DISTANCE
                       - lax.broadcasted_iota(jnp.int32, shape, 1))
    near = jnp.zeros(shape, F32)
    for b in range(N_BUCKETS):
        near = jnp.where(bucket == b, rb_ref[b, head], near)
    near_ref[0] = near * LOG2E

    @pl.when(head == 0)
    def _():
        n_q = n_heads * ls

        def slots(shape):
            row = lax.broadcasted_iota(jnp.int32, shape, 0)
            lane = lax.broadcasted_iota(jnp.int32, shape, 1)
            valid = jnp.logical_and(_rem(row, n_heads) == _div(_rem(lane, n_q), ls), lane < 2 * n_q)
            return _div(row, n_heads), _rem(lane, ls), valid

        def paged(ref, offset):
            key, query, valid = slots(ref.shape)
            bucket = bucket_of(query + offset - key)
            bias = jnp.zeros(ref.shape, F32)
            for b in range(N_BUCKETS):
                bias = jnp.where(bucket == b, rbl_ref[b:b + 1, :], bias)
            ref[...] = jnp.where(valid, bias * LOG2E, NEG_INF)

        paged(new_ref, 0)
        paged(tail_ref, PAGE_SIZE)
        tile = (V7X_SUBLANES, V7X_LANES)
        _, _, valid = slots(tile)
        far = jnp.broadcast_to(rbl_ref[N_BUCKETS - 1:N_BUCKETS, :] * LOG2E, tile)
        far_ref[0:8] = jnp.where(valid, far, NEG_INF)
        far_ref[8:16] = jnp.where(valid, far, 0.0)
        far_ref[16:24] = jnp.where(valid, NEG_INF, 0.0)
        far_ref[24:32] = jnp.where(valid, 1.0, 0.0)


def _bias_tables(rel_bias, tq, ls):
    n_heads = rel_bias.shape[1]
    n_q = n_heads * ls
    assert n_heads == V7X_SUBLANES and 2 * n_q <= V7X_LANES
    near_w = tq + MAX_DISTANCE
    page_rows = PAGE_SIZE * n_heads
    new_rows = -(-n_q // V7X_LANES) * V7X_LANES
    lane_head = (jnp.arange(V7X_LANES) % n_q) // ls
    rb_lanes = jnp.where(jnp.arange(V7X_LANES) < 2 * n_q, rel_bias[:, lane_head], 0.0)
    return pl.pallas_call(
        functools.partial(_bias_tables_kernel, n_heads=n_heads, ls=ls),
        grid=(n_heads,),
        in_specs=[pl.BlockSpec(memory_space=pltpu.SMEM), _whole((N_BUCKETS, V7X_LANES), 1)],
        out_specs=[
            pl.BlockSpec((1, tq, near_w), lambda h: (h, 0, 0)),
            _whole((new_rows, V7X_LANES), 1), _whole((page_rows, V7X_LANES), 1),
            _whole((4 * V7X_SUBLANES, V7X_LANES), 1),
        ],
        out_shape=[
            jax.ShapeDtypeStruct((n_heads, tq, near_w), F32),
            jax.ShapeDtypeStruct((new_rows, V7X_LANES), F32),
            jax.ShapeDtypeStruct((page_rows, V7X_LANES), F32),
            jax.ShapeDtypeStruct((4 * V7X_SUBLANES, V7X_LANES), F32),
        ],
        compiler_params=_params("arbitrary"),
        name="bias_tables",
    )(rel_bias, rb_lanes)


def _gated_conv_chunk(xn, wu, wb, wc, wz, cw, t, halo0, halo1):
    u = _dot(xn, wu)
    gate_b = _dot(xn, wb)
    gate_c = _dot(xn, wc)
    z = _dot(xn, wz)
    cu = gate_c * u
    prev1 = jnp.where(t >= 1, pltpu.roll(cu, 1, 0), halo1)
    prev2 = jnp.where(t >= 2, pltpu.roll(cu, 2, 0), jnp.where(t == 1, halo1, halo0))
    conv = prev2 * cw[0:1] + prev1 * cw[1:2] + cu * cw[2:3]
    return cu, gate_b * conv * _silu(z)


def _conv_prompt_kernel(*refs, tiles_per_seq, from_input):
    if from_input:
        x_ref, meta_ref, *refs = refs
    else:
        h_ref, *refs = refs
    nw_ref, wu_ref, wb_ref, wc_ref, wz_ref, cw_ref, wo_ref, o_ref, tail_ref, xn_scr, carry_scr = refs
    i, j = pl.program_id(0), pl.program_id(1)
    tm, tn = o_ref.shape[0], wu_ref.shape[1]
    starts_seq = i % tiles_per_seq == 0

    def start_tile(h):
        xn_scr[...] = _rms_norm(h, nw_ref[...]).astype(BF16)
        o_ref[...] = h

    if from_input:
        @pl.when(jnp.logical_and(j == 0, starts_seq))
        def _():
            start_tile(jnp.concatenate([meta_ref[...], x_ref[0:tm - N_META, :]], axis=0))

        @pl.when(jnp.logical_and(j == 0, jnp.logical_not(starts_seq)))
        def _():
            start_tile(x_ref[...])
    else:
        @pl.when(j == 0)
        def _():
            start_tile(h_ref[...])

    @pl.when(jnp.logical_and(j == 0, starts_seq))
    def _():
        carry_scr[...] = jnp.zeros(carry_scr.shape, F32)

    t = lax.broadcasted_iota(jnp.int32, (tm, tn), 0)
    halo0 = carry_scr[j, V7X_SUBLANES - 2:V7X_SUBLANES - 1, :]
    halo1 = carry_scr[j, V7X_SUBLANES - 1:V7X_SUBLANES, :]
    cu, y = _gated_conv_chunk(xn_scr[...], wu_ref[...], wb_ref[...], wc_ref[...], wz_ref[...], cw_ref[...],
                              t, halo0, halo1)
    last_rows = cu[tm - V7X_SUBLANES:tm]
    carry_scr[j] = last_rows
    tail_ref[0] = last_rows
    o_ref[...] += _dot(y.astype(BF16), wo_ref[...])


def _conv_sample_kernel(h_ref, nw_ref, wu_ref, wb_ref, wc_ref, wz_ref, cw_ref, wo_ref, halo0_ref, halo1_ref,
                        o_ref, cu_ref, wu_out, wb_out, wc_out, wz_out, wo_out, xn_scr, *, seg_rows):
    j = pl.program_id(0)
    tm, tn = h_ref.shape[0], wu_ref.shape[1]

    @pl.when(j == 0)
    def _():
        h = h_ref[...]
        xn_scr[...] = _rms_norm(h, nw_ref[...]).astype(BF16)
        o_ref[...] = h

    wu, wb, wc, wz, wo = _to_bf16((wu_ref, wb_ref, wc_ref, wz_ref, wo_ref), (wu_out, wb_out, wc_out, wz_out, wo_out))
    t = _rem(lax.broadcasted_iota(jnp.int32, (tm, tn), 0), seg_rows)
    cu, y = _gated_conv_chunk(xn_scr[...], wu, wb, wc, wz, cw_ref[...], t, halo0_ref[...], halo1_ref[...])
    cu_ref[...] = cu
    o_ref[...] += _dot(y.astype(BF16), wo)


def _w_in_specs(layer, d, tn, col_map):
    n_chunks = d // tn
    return [
        pl.BlockSpec((None, d, tn), functools.partial(col_map, layer, c * n_chunks)) for c in range(4)
    ]


def _conv_layer_prompt(h, norm_w, w_in_groups, conv_w, w_out, layer, batch, seq_len):
    from_input = isinstance(h, tuple)
    d = w_out.shape[0]
    n_rows = batch * seq_len
    tm, tn = _row_tile(seq_len, ROW_TILE_CAP), COL_TILE
    tiles_per_seq = seq_len // tm
    n_tiles, n_chunks = n_rows // tm, d // tn
    if from_input:
        x_len = seq_len - N_META
        assert tm > N_META and tiles_per_seq * tm - N_META == x_len

        assert x_len % V7X_BF16_SUBLANES == 0 and N_META % V7X_BF16_SUBLANES == 0

        def x_window(i, j):
            row = (i // tiles_per_seq) * x_len + jnp.maximum((i % tiles_per_seq) * tm - N_META, 0)
            return pl.multiple_of(row, V7X_BF16_SUBLANES), 0

        h_specs = [pl.BlockSpec((pl.Element(tm), pl.Element(d)), x_window), _whole((N_META, d), 2)]
        h_args = list(h)
    else:
        h_specs = [pl.BlockSpec((tm, d), lambda i, j: (i, 0))]
        h_args = [h]
    w_chunk = pl.BlockSpec((d, tn), lambda i, j: (0, j))
    out, tails = pl.pallas_call(
        functools.partial(_conv_prompt_kernel, tiles_per_seq=tiles_per_seq, from_input=from_input),
        grid=(n_tiles, n_chunks),
        in_specs=[
            *h_specs,
            _whole((1, d), 2),
            w_chunk, w_chunk, w_chunk, w_chunk,
            pl.BlockSpec((None, CONV_WIDTH, tn), lambda i, j: (layer, 0, j)),
            pl.BlockSpec((tn, d), lambda i, j: (j, 0)),
        ],
        out_specs=[
            pl.BlockSpec((tm, d), lambda i, j: (i, 0)),
            pl.BlockSpec((1, V7X_SUBLANES, tn), lambda i, j: (i, 0, j)),
        ],
        out_shape=[
            jax.ShapeDtypeStruct((n_rows, d), F32),
            jax.ShapeDtypeStruct((n_tiles, V7X_SUBLANES, d), F32),
        ],
        scratch_shapes=[
            pltpu.VMEM((tm, d), BF16),
            pltpu.VMEM((n_chunks, V7X_SUBLANES, tn), F32),
        ],
        compiler_params=_params("arbitrary", "arbitrary"),
        name="conv_mixer_prompt",
    )(*h_args, norm_w, *w_in_groups, conv_w, w_out)
    state = tails.reshape(batch, tiles_per_seq, V7X_SUBLANES, d)[:, -1, V7X_SUBLANES - (CONV_WIDTH - 1):]
    return out, state


def _conv_layer_sample(h, norm_w, w_in, conv_w, w_out, layer, state, seq_len):
    n_rows, d = h.shape
    tn = COL_TILE
    n_chunks = d // tn
    n_seq = n_rows // seq_len
    halo0 = jnp.repeat(state[:, 0], seq_len, axis=0)
    halo1 = jnp.repeat(state[:, 1], seq_len, axis=0)
    rows_chunk = pl.BlockSpec((n_rows, tn), lambda j: (0, j))
    w_chunk = pl.BlockSpec((d, tn), lambda j: (0, j))
    w_bf16 = jax.ShapeDtypeStruct((d, d), BF16)
    out, cu, *weights = pl.pallas_call(
        functools.partial(_conv_sample_kernel, seg_rows=seq_len),
        grid=(n_chunks,),
        in_specs=[
            _whole((n_rows, d), 1),
            _whole((1, d), 1),
            *_w_in_specs(layer, d, tn, lambda l, c0, j: (l, 0, c0 + j)),
            pl.BlockSpec((None, CONV_WIDTH, tn), lambda j: (layer, 0, j)),
            pl.BlockSpec((None, tn, d), lambda j: (layer, j, 0)),
            rows_chunk, rows_chunk,
        ],
        out_specs=[
            _whole((n_rows, d), 1), rows_chunk,
            w_chunk, w_chunk, w_chunk, w_chunk,
            pl.BlockSpec((tn, d), lambda j: (j, 0)),
        ],
        out_shape=[
            jax.ShapeDtypeStruct((n_rows, d), F32),
            jax.ShapeDtypeStruct((n_rows, d), F32),
            w_bf16, w_bf16, w_bf16, w_bf16, w_bf16,
        ],
        scratch_shapes=[pltpu.VMEM((n_rows, d), BF16)],
        compiler_params=_params("arbitrary"),
        name="conv_mixer_sample",
    )(h, norm_w, w_in, w_in, w_in, w_in, conv_w, w_out, halo0, halo1)
    new_state = cu.reshape(n_seq, seq_len, d)[:, seq_len - (CONV_WIDTH - 1):]
    return out, new_state, weights[:4], weights[4]


def _attn_inproj_sample_kernel(h_ref, nw_ref, wq_ref, wk_ref, wv_ref, wz_ref,
                               q_ref, k_ref, v_ref, z_ref, wq_out, wk_out, wv_out, wz_out, xn_scr):
    @pl.when(pl.program_id(0) == 0)
    def _():
        xn_scr[...] = _rms_norm(h_ref[...], nw_ref[...]).astype(BF16)

    wq, wk, wv, wz = _to_bf16((wq_ref, wk_ref, wv_ref, wz_ref), (wq_out, wk_out, wv_out, wz_out))
    xn = xn_scr[...]
    q_ref[...] = _dot(xn, wq) * SCORE_SCALE
    k_ref[...] = _dot(xn, wk)
    v_ref[...] = _dot(xn, wv)
    z_ref[...] = _dot(xn, wz)


def _attn_inproj_sample(h, norm_w, w_in, layer):
    n_rows, d = h.shape
    tn = COL_TILE
    chunk = pl.BlockSpec((n_rows, tn), lambda j: (0, j))
    w_chunk = pl.BlockSpec((d, tn), lambda j: (0, j))
    outs = pl.pallas_call(
        _attn_inproj_sample_kernel,
        grid=(d // tn,),
        in_specs=[_whole((n_rows, d), 1), _whole((1, d), 1),
                  *_w_in_specs(layer, d, tn, lambda l, c0, j: (l, 0, c0 + j))],
        out_specs=[chunk] * 4 + [w_chunk] * 4,
        out_shape=[jax.ShapeDtypeStruct((n_rows, d), F32)] * 4 + [jax.ShapeDtypeStruct((d, d), BF16)] * 4,
        scratch_shapes=[pltpu.VMEM((n_rows, d), BF16)],
        compiler_params=_params("arbitrary"),
        name="attn_inproj_sample",
    )(h, norm_w, w_in, w_in, w_in, w_in)
    return outs[:4], outs[4:]


def _attn_inproj_prompt_kernel(h_ref, nw_ref, wq_ref, wk_ref, wv_ref, wz_ref, *rest, n_earlier):
    j = pl.program_id(1)
    if n_earlier:
        earlier, rest = rest[:2 * n_earlier], rest[2 * n_earlier:]
        q_ref, kb_ref, vb_ref, z_ref, k_all, v_all, xn_scr, k_buf, v_buf, copy_sem, tile_sem = rest
    else:
        q_ref, kb_ref, vb_ref, z_ref, k_ref, v_ref, xn_scr = rest

    @pl.when(j == 0)
    def _():
        xn_scr[...] = _rms_norm(h_ref[...], nw_ref[...]).astype(BF16)

    xn = xn_scr[...]
    q_ref[...] = (_dot(xn, wq_ref[...]) * SCORE_SCALE).astype(BF16)
    z_ref[...] = _dot(xn, wz_ref[...]).astype(BF16)
    k = _dot(xn, wk_ref[...])
    v = _dot(xn, wv_ref[...])
    kb_ref[...] = k.astype(BF16)
    vb_ref[...] = v.astype(BF16)
    if not n_earlier:
        k_ref[:, j, :] = k
        v_ref[:, j, :] = v
        return

    i = pl.program_id(0)
    n_tiles, n_chunks = pl.num_programs(0), pl.num_programs(1)
    tm = k_buf.shape[1]
    slot = lax.rem(i, 2)

    def earlier_copies():
        for layer in range(n_earlier):
            for c, (src, dst) in enumerate(((earlier[2 * layer], k_all), (earlier[2 * layer + 1], v_all))):
                yield pltpu.make_async_copy(src, dst.at[layer], copy_sem.at[layer, c])

    def tile_copies(tile, tile_slot):
        rows = pl.ds(pl.multiple_of(tile * tm, V7X_SUBLANES), tm)
        for c, (buf, dst) in enumerate(((k_buf, k_all), (v_buf, v_all))):
            yield pltpu.make_async_copy(buf.at[tile_slot], dst.at[n_earlier, rows], tile_sem.at[tile_slot, c])

    @pl.when(jnp.logical_and(i == 0, j == 0))
    def _():
        for copy in earlier_copies():
            copy.start()

    @pl.when(jnp.logical_and(i >= 2, j == 0))
    def _():
        for copy in tile_copies(i - 2, slot):
            copy.wait()

    k_buf[slot, :, j, :] = k
    v_buf[slot, :, j, :] = v

    @pl.when(j == n_chunks - 1)
    def _():
        for copy in tile_copies(i, slot):
            copy.start()

    @pl.when(jnp.logical_and(i == n_tiles - 1, j == n_chunks - 1))
    def _():
        for copy in tile_copies(i, slot):
            copy.wait()
        if k_all.shape[1] > tm:
            for copy in tile_copies(i - 1, 1 - slot):
                copy.wait()
        for copy in earlier_copies():
            copy.wait()


def _attn_inproj_prompt(h, norm_w, weights, tm, earlier_kv=()):
    n_rows, d = h.shape
    tn = COL_TILE
    assert tn == HEAD_WIDTH
    n_heads = d // tn
    n_earlier = len(earlier_kv)
    chunk = pl.BlockSpec((tm, tn), lambda i, j: (i, j))
    w_chunk = pl.BlockSpec((d, tn), lambda i, j: (0, j))
    act = jax.ShapeDtypeStruct((n_rows, d), BF16)
    in_specs = [pl.BlockSpec((tm, d), lambda i, j: (i, 0)), _whole((1, d), 2), w_chunk, w_chunk, w_chunk, w_chunk]
    scratch = [pltpu.VMEM((tm, d), BF16)]
    if n_earlier:
        in_specs += [pl.BlockSpec(memory_space=pl.ANY)] * (2 * n_earlier)
        kv_spec = pl.BlockSpec(memory_space=pl.ANY)
        kv = jax.ShapeDtypeStruct((n_earlier + 1, n_rows, n_heads, tn), F32)
        scratch += [
            pltpu.VMEM((2, tm, n_heads, tn), F32),
            pltpu.VMEM((2, tm, n_heads, tn), F32),
            pltpu.SemaphoreType.DMA((n_earlier, 2)),
            pltpu.SemaphoreType.DMA((2, 2)),
        ]
    else:
        kv_spec = pl.BlockSpec((tm, n_heads, tn), lambda i, j: (i, 0, 0))
        kv = jax.ShapeDtypeStruct((n_rows, n_heads, tn), F32)
    return pl.pallas_call(
        functools.partial(_attn_inproj_prompt_kernel, n_earlier=n_earlier),
        grid=(n_rows // tm, d // tn),
        in_specs=in_specs,
        out_specs=[chunk, chunk, chunk, chunk, kv_spec, kv_spec],
        out_shape=[act, act, act, act, kv, kv],
        scratch_shapes=scratch,
        compiler_params=_params("arbitrary", "arbitrary"),
        name="attn_inproj_prompt",
    )(h, norm_w, *weights, *[a for pair in earlier_kv for a in pair])


def _prompt_attn_kernel(rb_ref, q_ref, k_ref, v_ref, z_ref, near_ref, lam_ref, sw_ref,
                        y_ref, kb_scr, vb_scr, *, lam_init):
    head = pl.program_id(1)
    seq_len = q_ref.shape[1]
    padded_len = kb_scr.shape[0]
    tq = near_ref.shape[1]
    kb_scr[0:seq_len] = k_ref[0]
    vb_scr[0:seq_len] = v_ref[0]
    if padded_len > seq_len:
        kb_scr[seq_len:padded_len] = jnp.zeros((padded_len - seq_len, HEAD_WIDTH), BF16)
        vb_scr[seq_len:padded_len] = jnp.zeros((padded_len - seq_len, HEAD_WIDTH), BF16)
    lam = _lambda(lam_ref[...], lam_init)
    far_bias = rb_ref[N_BUCKETS - 1, head] * LOG2E

    for r0 in range(0, seq_len, tq):
        rows = min(tq, seq_len - r0)
        near_lo = max(r0 - MAX_DISTANCE, 0)
        near_hi = r0 + tq
        near_w = near_hi - near_lo
        col0 = near_lo - (r0 - MAX_DISTANCE)
        q = q_ref[0, r0:r0 + rows, :]
        row = lax.broadcasted_iota(jnp.int32, (rows, near_w), 0)
        col = lax.broadcasted_iota(jnp.int32, (rows, near_w), 1)
        visible = col + col0 <= row + MAX_DISTANCE
        near_bias = near_ref[0, 0:rows, col0:col0 + near_w]
        e_near, e_far, denom = [], [], []
        for c in range(2):
            lanes = slice(c * HEAD_DIM, (c + 1) * HEAD_DIM)
            s_near = _dot_nt(q[:, lanes], kb_scr[near_lo:near_hi, lanes])
            s_near = jnp.where(visible, s_near + near_bias, NEG_INF)
            m = jnp.max(s_near, axis=-1, keepdims=True)
            if near_lo > 0:
                s_far = _dot_nt(q[:, lanes], kb_scr[0:near_lo, lanes])
                m = jnp.maximum(m, jnp.max(s_far, axis=-1, keepdims=True) + far_bias)
                e_far.append(jnp.exp2(s_far - (m - far_bias)))
            e_near.append(jnp.exp2(s_near - m))
            total = jnp.sum(e_near[c], axis=-1, keepdims=True)
            if near_lo > 0:
                total = total + jnp.sum(e_far[c], axis=-1, keepdims=True)
            denom.append(total)
        w1 = 1.0 / denom[0]
        w2 = lam / denom[1]
        a_near = e_near[0] * w1 - e_near[1] * w2
        o = _dot(a_near.astype(BF16), vb_scr[near_lo:near_hi, :])
        if near_lo > 0:
            a_far = e_far[0] * w1 - e_far[1] * w2
            o = o + _dot(a_far.astype(BF16), vb_scr[0:near_lo, :])
        z = z_ref[0, r0:r0 + rows, :].astype(F32)
        y_ref[0, r0:r0 + rows, :] = _sub_norm_gate(o, sw_ref[...], lam_init, z).astype(y_ref.dtype)


def _prompt_attention(q, k, v, z, rel_bias, near_table, lam_params, sub_w, lam_init):
    batch, seq_len, d = q.shape
    n_heads = d // HEAD_WIDTH
    tq = near_table.shape[1]
    padded_len = -(-seq_len // tq) * tq
    head_cols = pl.BlockSpec((1, seq_len, HEAD_WIDTH), lambda b, h: (b, 0, h))
    return pl.pallas_call(
        functools.partial(_prompt_attn_kernel, lam_init=lam_init),
        grid=(batch, n_heads),
        in_specs=[
            pl.BlockSpec(memory_space=pltpu.SMEM),
            head_cols, head_cols, head_cols, head_cols,
            pl.BlockSpec((1, tq, tq + MAX_DISTANCE), lambda b, h: (h, 0, 0)),
            _whole((4, HEAD_DIM), 2),
            _whole((1, HEAD_WIDTH), 2),
        ],
        out_specs=head_cols,
        out_shape=jax.ShapeDtypeStruct((batch, seq_len, d), BF16),
        scratch_shapes=[
            pltpu.VMEM((padded_len, HEAD_WIDTH), BF16),
            pltpu.VMEM((padded_len, HEAD_WIDTH), BF16),
        ],
        compiler_params=_params("arbitrary", "arbitrary"),
        name="prompt_attention",
    )(rel_bias, q, k, v, z, near_table, lam_params, sub_w)


def _decode_attn_kernel(pt_ref, q_ref, kn_ref, vn_ref, z_ref, *rest, n_pg, lam_init):
    k_pages, v_pages = rest[:n_pg], rest[n_pg:2 * n_pg]
    (new_ref, tail_ref, far_ref, lam_ref, sw_ref, y_ref,
     s_scr, m_seen_scr, m_used_scr, l_scr, acc_scr) = rest[2 * n_pg:]
    del pt_ref
    step, n_groups = pl.program_id(1), pl.num_programs(1) - 1
    ls, d = z_ref.shape
    n_heads = d // HEAD_WIDTH
    n_q = n_heads * ls
    page_rows = PAGE_SIZE * n_heads
    tile = (V7X_SUBLANES, V7X_LANES)
    q_cols = q_ref[0]
    far_bias, far_bias_or_0, m_start, valid = far_ref[0:8], far_ref[8:16], far_ref[16:24], far_ref[24:32]
    slot = lax.rem(step, 2)

    def per_tile(x):
        return x.reshape(x.shape[0] // V7X_SUBLANES, *tile)

    def lanes_to_rows(stat):
        per_lane = jnp.sum(stat * valid, axis=0, keepdims=True)
        return jnp.broadcast_to(per_lane, tile).T[:, 0:1]

    def fold(probs, values, m_from, m_to):
        alpha = jnp.exp2(m_from - m_to)
        total = alpha * l_scr[...]
        pv = None
        for p, vb in zip(probs, values):
            total = total + jnp.sum(p, axis=0)
            part = lax.dot_general(p.reshape(vb.shape[0], V7X_LANES), vb, (((0,), (0,)), ((), ())),
                                   preferred_element_type=F32)
            pv = part if pv is None else pv + part
        l_scr[...] = total
        acc_scr[...] = lanes_to_rows(alpha) * acc_scr[...] + pv

    @pl.when(step == 0)
    def _():
        m_seen_scr[...] = m_start
        m_used_scr[...] = m_start
        l_scr[...] = jnp.zeros(l_scr.shape, F32)
        acc_scr[...] = jnp.zeros(acc_scr.shape, F32)
        s_scr[1] = jnp.full(s_scr.shape[1:], NEG_INF, F32)

    keys_are_tail = step >= n_groups - 1
    values_are_tail = step == n_groups

    m_use = m_seen_scr[...]
    prev = s_scr.at[1 - slot]
    probs = []
    for r in range(n_pg):
        s = per_tile(prev[r * page_rows:(r + 1) * page_rows])
        if r == n_pg - 1:
            probs.append(jnp.exp2(s + jnp.where(values_are_tail, per_tile(tail_ref[...]), far_bias) - m_use))
        else:
            probs.append(jnp.exp2(s + (far_bias - m_use)))
    fold(probs, [ref[0, 0].reshape(page_rows, HEAD_WIDTH) for ref in v_pages], m_used_scr[...], m_use)
    m_used_scr[...] = m_use

    m_new = m_use
    for r, ref in enumerate(k_pages):
        logits = _dot(ref[0, 0].reshape(page_rows, HEAD_WIDTH), q_cols)
        s_scr[slot, r * page_rows:(r + 1) * page_rows] = logits
        s = per_tile(logits)
        if r == n_pg - 1:
            s = s + jnp.where(keys_are_tail, per_tile(tail_ref[...]), far_bias_or_0)
            m_new = jnp.maximum(m_new, jnp.max(s, axis=0))
        else:
            m_new = jnp.maximum(m_new, jnp.max(s, axis=0) + far_bias_or_0)
    m_seen_scr[...] = m_new

    @pl.when(step == n_groups)
    def _():
        shape = new_ref.shape
        key = _div(lax.broadcasted_iota(jnp.int32, shape, 0), n_heads)
        query = _rem(lax.broadcasted_iota(jnp.int32, shape, 1), ls)
        s = per_tile(_dot(kn_ref[0], q_cols) + jnp.where(key <= query, new_ref[...], NEG_INF))
        m_all = jnp.maximum(m_use, jnp.max(s, axis=0))
        fold([jnp.exp2(s - m_all)], [vn_ref[0]], m_use, m_all)
        lam = _lambda(lam_ref[...], lam_init)
        o = acc_scr[...] / lanes_to_rows(l_scr[...])
        o = o[0:n_q] - lam * o[n_q:2 * n_q]
        o = o * lax.rsqrt(jnp.mean(o * o, axis=-1, keepdims=True) + EPS)
        o = o * sw_ref[...] * (1.0 - lam_init)
        for h in range(n_heads):
            cols = slice(h * HEAD_WIDTH, (h + 1) * HEAD_WIDTH)
            y_ref[:, cols] = o[h * ls:(h + 1) * ls] * _silu(z_ref[:, cols])


def _decode_attention(q, k_new, v_new, z, cache_k, cache_v, page_table, layer, new_table, tail_table, far_table,
                      lam_params, sub_w, lam_init, ls):
    n_rows, d = q.shape
    batch, n_pages = page_table.shape
    n_heads = d // HEAD_WIDTH
    n_q = n_heads * ls
    new_rows = new_table.shape[0]
    n_pg = math.gcd(n_pages, PAGES_PER_STEP)
    n_groups = n_pages // n_pg
    q_t = q.reshape(batch, ls, n_heads, 2, HEAD_DIM).transpose(0, 3, 4, 2, 1).reshape(batch, 2, HEAD_DIM, n_q)
    q_cols = jnp.zeros((batch, HEAD_WIDTH, V7X_LANES), F32)
    q_cols = q_cols.at[:, :HEAD_DIM, :n_q].set(q_t[:, 0]).at[:, HEAD_DIM:, n_q:2 * n_q].set(q_t[:, 1])
    pad = ((0, 0), (0, new_rows - n_q), (0, 0))
    k_rows = jnp.pad(k_new.reshape(batch, n_q, HEAD_WIDTH), pad)
    v_rows = jnp.pad(v_new.reshape(batch, n_q, HEAD_WIDTH), pad)

    def page_specs(group_of_step):
        return [
            pl.BlockSpec((1, 1, PAGE_SIZE, n_heads, HEAD_WIDTH), functools.partial(
                lambda r, b, s, pt: (layer, pt[b, group_of_step(s) * n_pg + r], 0, 0, 0), r))
            for r in range(n_pg)
        ]

    new_page = pl.BlockSpec((1, new_rows, HEAD_WIDTH), lambda b, s, pt: (b, 0, 0))
    grid_spec = pltpu.PrefetchScalarGridSpec(
        num_scalar_prefetch=1,
        grid=(batch, n_groups + 1),
        in_specs=[
            pl.BlockSpec((1, HEAD_WIDTH, V7X_LANES), lambda b, s, pt: (b, 0, 0)),
            new_page, new_page,
            pl.BlockSpec((ls, d), lambda b, s, pt: (b, 0)),
            *page_specs(lambda s: jnp.minimum(s, n_groups - 1)), *page_specs(lambda s: jnp.maximum(s - 1, 0)),
            _whole(new_table.shape, 3), _whole(tail_table.shape, 3), _whole(far_table.shape, 3),
            _whole((4, HEAD_DIM), 3), _whole((1, HEAD_WIDTH), 3),
        ],
        out_specs=pl.BlockSpec((ls, d), lambda b, s, pt: (b, 0)),
        scratch_shapes=[
            pltpu.VMEM((2, n_pg * PAGE_SIZE * n_heads, V7X_LANES), F32),
            pltpu.VMEM((V7X_SUBLANES, V7X_LANES), F32),
            pltpu.VMEM((V7X_SUBLANES, V7X_LANES), F32),
            pltpu.VMEM((V7X_SUBLANES, V7X_LANES), F32),
            pltpu.VMEM((V7X_LANES, HEAD_WIDTH), F32),
        ],
    )
    return pl.pallas_call(
        functools.partial(_decode_attn_kernel, n_pg=n_pg, lam_init=lam_init),
        grid_spec=grid_spec,
        out_shape=jax.ShapeDtypeStruct((n_rows, d), F32),
        compiler_params=_params("arbitrary", "arbitrary"),
        name="decode_attention",
    )(page_table, q_cols, k_rows, v_rows, z, *([cache_k] * n_pg), *([cache_v] * n_pg),
      new_table, tail_table, far_table, lam_params, sub_w)


def _outproj_kernel(y_ref, w_ref, h_ref, *rest, final_norm):
    h = h_ref[...] + _dot(y_ref[...], w_ref[...])
    if final_norm:
        fw_ref, o_ref = rest
        o_ref[...] = _rms_norm(h, fw_ref[...])
    else:
        (o_ref,) = rest
        o_ref[...] = h


def _outproj_prompt(y, w_out, h, batch, seq_len, final_w=None):
    n_rows, d = h.shape
    w_spec = _whole((d, d), 1)
    if final_w is None:
        tm = _row_tile(seq_len, ROW_TILE_CAP)
        rows = pl.BlockSpec((tm, d), lambda i: (i, 0))
        return pl.pallas_call(
            functools.partial(_outproj_kernel, final_norm=False),
            grid=(n_rows // tm,),
            in_specs=[rows, w_spec, rows],
            out_specs=rows,
            out_shape=jax.ShapeDtypeStruct((n_rows, d), F32),
            compiler_params=_params("arbitrary"),
            name="attn_outproj",
        )(y, w_out, h)
    out_len = seq_len - N_META
    tm = _row_tile(out_len, ROW_TILE_CAP)
    tiles_per_seq = out_len // tm
    assert seq_len % V7X_BF16_SUBLANES == 0 and N_META % V7X_BF16_SUBLANES == 0

    def rows_window(i):
        row = (i // tiles_per_seq) * seq_len + N_META + (i % tiles_per_seq) * tm
        return pl.multiple_of(row, V7X_BF16_SUBLANES), 0

    window = pl.BlockSpec((pl.Element(tm), pl.Element(d)), rows_window)
    return pl.pallas_call(
        functools.partial(_outproj_kernel, final_norm=True),
        grid=(batch * tiles_per_seq,),
        in_specs=[window, w_spec, window, _whole((1, d), 1)],
        out_specs=pl.BlockSpec((tm, d), lambda i: (i, 0)),
        out_shape=jax.ShapeDtypeStruct((batch * out_len, d), F32),
        compiler_params=_params("arbitrary"),
        name="attn_outproj_final",
    )(y, w_out, h, final_w)


def _outproj_sample_kernel(y_ref, w_ref, h_ref, o_ref, w_out):
    (w,) = _to_bf16((w_ref,), (w_out,))
    o_ref[...] = h_ref[...] + _dot(y_ref[...].astype(BF16), w)


def _outproj_sample(y, w_out, layer, h):
    n_rows, d = h.shape
    tn = OUT_COL_TILE
    cols = pl.BlockSpec((n_rows, tn), lambda j: (0, j))
    return pl.pallas_call(
        _outproj_sample_kernel,
        grid=(d // tn,),
        in_specs=[_whole((n_rows, d), 1), pl.BlockSpec((None, d, tn), lambda j: (layer, 0, j)), cols],
        out_specs=[cols, pl.BlockSpec((d, tn), lambda j: (0, j))],
        out_shape=[jax.ShapeDtypeStruct((n_rows, d), F32), jax.ShapeDtypeStruct((d, d), BF16)],
        compiler_params=_params("arbitrary"),
        name="attn_outproj_sample",
    )(y, w_out, h)


def _final_norm_kernel(h_ref, w_ref, o_ref):
    o_ref[...] = _rms_norm(h_ref[...], w_ref[...])


def _final_norm(h, w):
    return pl.pallas_call(
        _final_norm_kernel, out_shape=jax.ShapeDtypeStruct(h.shape, F32), name="final_norm_sample")(h, w)


def kernel(x_prompt, x_sample, state_conv, cache_k, cache_v, page_table, meta_tokens, rel_bias, norm_w,
           final_norm_w, conv_w_in, conv_w, conv_w_out, attn_w_in, attn_lambda, attn_subln_w, attn_w_out):
    b, seq, d = x_prompt.shape
    db, ls, _ = x_sample.shape
    depth = norm_w.shape[0]
    lp = N_META + seq
    n_heads = d // HEAD_WIDTH
    assert d % HEAD_WIDTH == 0 and d % COL_TILE == 0 and d % OUT_COL_TILE == 0
    assert ls >= CONV_WIDTH - 1 and (db * ls) % V7X_BF16_SUBLANES == 0
    assert cache_k.shape[2:] == (PAGE_SIZE, n_heads, HEAD_WIDTH) and cache_v.shape == cache_k.shape
    assert ls <= PAGE_SIZE
    assert depth % 2 == 0, "the final norm is fused into the last attention mixer's out-projection"

    hp = (x_prompt.reshape(b * seq, d), meta_tokens.astype(x_prompt.dtype))
    hs = x_sample.reshape(db * ls, d)
    tm_p = _row_tile(lp, ROW_TILE_CAP)
    fw = final_norm_w.reshape(1, d)

    near_table, new_table, tail_table, far_table = _bias_tables(rel_bias, Q_TILE, ls)

    conv_p, conv_s, k_s, v_s = [], [], [], []
    kv_p = []
    for i in range(depth):
        nw = norm_w[i].reshape(1, d)
        j = i // 2
        last = i == depth - 1
        if i % 2 == 0:
            hs, st_s, w_in_b, w_out_b = _conv_layer_sample(hs, nw, conv_w_in, conv_w, conv_w_out, j, state_conv[j], ls)
            hp, st_p = _conv_layer_prompt(hp, nw, w_in_b, conv_w, w_out_b, j, b, lp)
            conv_p.append(st_p)
            conv_s.append(st_s)
        else:
            lam_init = 0.8 - 0.6 * math.exp(-0.3 * i)
            sub_w = attn_subln_w[j].reshape(1, HEAD_WIDTH)
            (qs, ks, vs, zs), w_in_b = _attn_inproj_sample(hs, nw, attn_w_in, j)
            ys = _decode_attention(qs, ks, vs, zs, cache_k, cache_v, page_table, j, new_table, tail_table,
                                   far_table, attn_lambda[j], sub_w, lam_init, ls)
            hs, w_out_b = _outproj_sample(ys, attn_w_out, j, hs)
            last_attn = j == depth // 2 - 1
            q, k, v, z, k_out, v_out = _attn_inproj_prompt(hp, nw, w_in_b, tm_p, kv_p if last_attn else ())
            kv_p = [(k_out, v_out)] if last_attn else kv_p + [(k_out, v_out)]
            yp = _prompt_attention(q.reshape(b, lp, d), k.reshape(b, lp, d), v.reshape(b, lp, d),
                                   z.reshape(b, lp, d), rel_bias, near_table, attn_lambda[j], sub_w, lam_init)
            hp = _outproj_prompt(yp.reshape(b * lp, d), w_out_b, hp, b, lp, fw if last else None)
            k_s.append(ks.reshape(db, ls, n_heads, HEAD_WIDTH))
            v_s.append(vs.reshape(db, ls, n_heads, HEAD_WIDTH))

    y_prompt = hp.reshape(b, seq, d)
    y_sample = _final_norm(hs, fw).reshape(db, ls, d)
    k_p, v_p = (a.reshape(depth // 2, b, lp, n_heads, HEAD_WIDTH) for a in kv_p[0])
    return (y_prompt, y_sample, jnp.stack(conv_p), jnp.stack(conv_s), k_p, v_p, jnp.stack(k_s), jnp.stack(v_s))
```

```python
import functools
import math

import jax
import jax.numpy as jnp
from jax import lax
from jax.experimental import pallas as pl
from jax.experimental.pallas import tpu as pltpu

N_META = 16
HEAD_DIM = 128
HEAD_WIDTH = 2 * HEAD_DIM
PAGE_SIZE = 128
N_BUCKETS = 32
MAX_EXACT = N_BUCKETS // 2
MAX_DISTANCE = 128
CONV_WIDTH = 3
EPS = 1e-6
NEG_INF = -1e30
LOG2E = math.log2(math.e)
SCORE_SCALE = HEAD_DIM ** -0.5 * LOG2E

BF16 = jnp.bfloat16
F32 = jnp.float32

V7X_LANES = 128
V7X_SUBLANES = 8
V7X_BF16_SUBLANES = 16
V7X_VMEM_LIMIT_BYTES = 56 * 1024 * 1024

ROW_TILE_CAP = 768
COL_TILE = 256
OUT_COL_TILE = 512
Q_TILE = 512
PAGES_PER_STEP = 4


def _row_tile(n_rows, cap):
    best = None
    for t in range(V7X_BF16_SUBLANES, min(n_rows, cap) + 1, V7X_BF16_SUBLANES):
        if n_rows % t == 0:
            best = t
    return n_rows if best is None else best


def _params(*semantics):
    return pltpu.CompilerParams(dimension_semantics=semantics, vmem_limit_bytes=V7X_VMEM_LIMIT_BYTES)


def _dot(a, b):
    return jnp.dot(a, b, preferred_element_type=F32)


def _dot_nt(a, b):
    return lax.dot_general(a, b, (((1,), (1,)), ((), ())), preferred_element_type=F32)


def _rms_norm(x, w):
    return x * lax.rsqrt(jnp.mean(x * x, axis=-1, keepdims=True) + EPS) * w


def _silu(z):
    return z * (1.0 / (1.0 + jnp.exp(-z)))


def _lambda(lp, lam_init):
    a = jnp.sum(lp[0:1] * lp[1:2], axis=-1, keepdims=True)
    b = jnp.sum(lp[2:3] * lp[3:4], axis=-1, keepdims=True)
    return jnp.exp(a) - jnp.exp(b) + lam_init


def _sub_norm_gate(o, sub_w, lam_init, z):
    o = o * lax.rsqrt(jnp.mean(o * o, axis=-1, keepdims=True) + EPS)
    o = o * sub_w * (1.0 - lam_init)
    return o * _silu(z)


def _div(x, n):
    return x >> (n.bit_length() - 1) if n & (n - 1) == 0 else lax.div(x, jnp.int32(n))


def _rem(x, n):
    return x & (n - 1) if n & (n - 1) == 0 else lax.rem(x, jnp.int32(n))


def _to_bf16(src_refs, dst_refs):
    out = []
    for src, dst in zip(src_refs, dst_refs):
        w = src[...].astype(BF16)
        dst[...] = w
        out.append(w)
    return out


def _whole(shape, n_grid_axes):
    return pl.BlockSpec(shape, lambda *_: (0,) * len(shape))


def _bias_tables_kernel(rb_ref, rbl_ref, near_ref, new_ref, tail_ref, far_ref, *, n_heads, ls):
    head = pl.program_id(0)

    def bucket_of(n):
        n = jnp.maximum(n, 0)
        nf = jnp.maximum(n, 1).astype(F32)
        large = MAX_EXACT + (
            jnp.log(nf / MAX_EXACT) / math.log(MAX_DISTANCE / MAX_EXACT) * (N_BUCKETS - MAX_EXACT)
        ).astype(jnp.int32)
        large = jnp.minimum(large, N_BUCKETS - 1)
        return jnp.where(n < MAX_EXACT, n, large)

    shape = near_ref.shape[1:]
    bucket = bucket_of(lax.broadcasted_iota(jnp.int32, shape, 0) + MAX_DISTANCE
                       - lax.broadcasted_iota(jnp.int32, shape, 1))
    near = jnp.zeros(shape, F32)
    for b in range(N_BUCKETS):
        near = jnp.where(bucket == b, rb_ref[b, head], near)
    near_ref[0] = near * LOG2E

    @pl.when(head == 0)
    def _():
        n_q = n_heads * ls

        def slots(shape):
            row = lax.broadcasted_iota(jnp.int32, shape, 0)
            lane = lax.broadcasted_iota(jnp.int32, shape, 1)
            valid = jnp.logical_and(_rem(row, n_heads) == _div(_rem(lane, n_q), ls), lane < 2 * n_q)
            return _div(row, n_heads), _rem(lane, ls), valid

        def paged(ref, offset):
            key, query, valid = slots(ref.shape)
            bucket = bucket_of(query + offset - key)
            bias = jnp.zeros(ref.shape, F32)
            for b in range(N_BUCKETS):
                bias = jnp.where(bucket == b, rbl_ref[b:b + 1, :], bias)
            ref[...] = jnp.where(valid, bias * LOG2E, NEG_INF)

        paged(new_ref, 0)
        paged(tail_ref, PAGE_SIZE)
        tile = (V7X_SUBLANES, V7X_LANES)
        _, _, valid = slots(tile)
        far = jnp.broadcast_to(rbl_ref[N_BUCKETS - 1:N_BUCKETS, :] * LOG2E, tile)
        far_ref[0:8] = jnp.where(valid, far, NEG_INF)
        far_ref[8:16] = jnp.where(valid, far, 0.0)
        far_ref[16:24] = jnp.where(valid, NEG_INF, 0.0)
        far_ref[24:32] = jnp.where(valid, 1.0, 0.0)


def _bias_tables(rel_bias, tq, ls):
    n_heads = rel_bias.shape[1]
    n_q = n_heads * ls
    assert n_heads == V7X_SUBLANES and 2 * n_q <= V7X_LANES
    near_w = tq + MAX_DISTANCE
    page_rows = PAGE_SIZE * n_heads
    new_rows = -(-n_q // V7X_LANES) * V7X_LANES
    lane_head = (jnp.arange(V7X_LANES) % n_q) // ls
    rb_lanes = jnp.where(jnp.arange(V7X_LANES) < 2 * n_q, rel_bias[:, lane_head], 0.0)
    return pl.pallas_call(
        functools.partial(_bias_tables_kernel, n_heads=n_heads, ls=ls),
        grid=(n_heads,),
        in_specs=[pl.BlockSpec(memory_space=pltpu.SMEM), _whole((N_BUCKETS, V7X_LANES), 1)],
        out_specs=[
            pl.BlockSpec((1, tq, near_w), lambda h: (h, 0, 0)),
            _whole((new_rows, V7X_LANES), 1), _whole((page_rows, V7X_LANES), 1),
            _whole((4 * V7X_SUBLANES, V7X_LANES), 1),
        ],
        out_shape=[
            jax.ShapeDtypeStruct((n_heads, tq, near_w), F32),
            jax.ShapeDtypeStruct((new_rows, V7X_LANES), F32),
            jax.ShapeDtypeStruct((page_rows, V7X_LANES), F32),
            jax.ShapeDtypeStruct((4 * V7X_SUBLANES, V7X_LANES), F32),
        ],
        compiler_params=_params("arbitrary"),
        name="bias_tables",
    )(rel_bias, rb_lanes)


def _gated_conv_chunk(xn, wu, wb, wc, wz, cw, t, halo0, halo1):
    u = _dot(xn, wu)
    gate_b = _dot(xn, wb)
    gate_c = _dot(xn, wc)
    z = _dot(xn, wz)
    cu = gate_c * u
    prev1 = jnp.where(t >= 1, pltpu.roll(cu, 1, 0), halo1)
    prev2 = jnp.where(t >= 2, pltpu.roll(cu, 2, 0), jnp.where(t == 1, halo1, halo0))
    conv = prev2 * cw[0:1] + prev1 * cw[1:2] + cu * cw[2:3]
    return cu, gate_b * conv * _silu(z)


def _conv_prompt_kernel(*refs, tiles_per_seq, from_input):
    if from_input:
        x_ref, meta_ref, *refs = refs
    else:
        h_ref, *refs = refs
    nw_ref, wu_ref, wb_ref, wc_ref, wz_ref, cw_ref, wo_ref, o_ref, tail_ref, xn_scr, carry_scr = refs
    i, j = pl.program_id(0), pl.program_id(1)
    tm, tn = o_ref.shape[0], wu_ref.shape[1]
    starts_seq = i % tiles_per_seq == 0

    def start_tile(h):
        xn_scr[...] = _rms_norm(h, nw_ref[...]).astype(BF16)
        o_ref[...] = h

    if from_input:
        @pl.when(jnp.logical_and(j == 0, starts_seq))
        def _():
            start_tile(jnp.concatenate([meta_ref[...], x_ref[0:tm - N_META, :]], axis=0))

        @pl.when(jnp.logical_and(j == 0, jnp.logical_not(starts_seq)))
        def _():
            start_tile(x_ref[...])
    else:
        @pl.when(j == 0)
        def _():
            start_tile(h_ref[...])

    @pl.when(jnp.logical_and(j == 0, starts_seq))
    def _():
        carry_scr[...] = jnp.zeros(carry_scr.shape, F32)

    t = lax.broadcasted_iota(jnp.int32, (tm, tn), 0)
    halo0 = carry_scr[j, V7X_SUBLANES - 2:V7X_SUBLANES - 1, :]
    halo1 = carry_scr[j, V7X_SUBLANES - 1:V7X_SUBLANES, :]
    cu, y = _gated_conv_chunk(xn_scr[...], wu_ref[...], wb_ref[...], wc_ref[...], wz_ref[...], cw_ref[...],
                              t, halo0, halo1)
    last_rows = cu[tm - V7X_SUBLANES:tm]
    carry_scr[j] = last_rows
    tail_ref[0] = last_rows
    o_ref[...] += _dot(y.astype(BF16), wo_ref[...])


def _conv_sample_kernel(h_ref, nw_ref, wu_ref, wb_ref, wc_ref, wz_ref, cw_ref, wo_ref, halo0_ref, halo1_ref,
                        o_ref, cu_ref, wu_out, wb_out, wc_out, wz_out, wo_out, xn_scr, *, seg_rows):
    j = pl.program_id(0)
    tm, tn = h_ref.shape[0], wu_ref.shape[1]

    @pl.when(j == 0)
    def _():
        h = h_ref[...]
        xn_scr[...] = _rms_norm(h, nw_ref[...]).astype(BF16)
        o_ref[...] = h

    wu, wb, wc, wz, wo = _to_bf16((wu_ref, wb_ref, wc_ref, wz_ref, wo_ref), (wu_out, wb_out, wc_out, wz_out, wo_out))
    t = _rem(lax.broadcasted_iota(jnp.int32, (tm, tn), 0), seg_rows)
    cu, y = _gated_conv_chunk(xn_scr[...], wu, wb, wc, wz, cw_ref[...], t, halo0_ref[...], halo1_ref[...])
    cu_ref[...] = cu
    o_ref[...] += _dot(y.astype(BF16), wo)


def _w_in_specs(layer, d, tn, col_map):
    n_chunks = d // tn
    return [
        pl.BlockSpec((None, d, tn), functools.partial(col_map, layer, c * n_chunks)) for c in range(4)
    ]


def _conv_layer_prompt(h, norm_w, w_in_groups, conv_w, w_out, layer, batch, seq_len):
    from_input = isinstance(h, tuple)
    d = w_out.shape[0]
    n_rows = batch * seq_len
    tm, tn = _row_tile(seq_len, ROW_TILE_CAP), COL_TILE
    tiles_per_seq = seq_len // tm
    n_tiles, n_chunks = n_rows // tm, d // tn
    if from_input:
        x_len = seq_len - N_META
        assert tm > N_META and tiles_per_seq * tm - N_META == x_len

        assert x_len % V7X_BF16_SUBLANES == 0 and N_META % V7X_BF16_SUBLANES == 0

        def x_window(i, j):
            row = (i // tiles_per_seq) * x_len + jnp.maximum((i % tiles_per_seq) * tm - N_META, 0)
            return pl.multiple_of(row, V7X_BF16_SUBLANES), 0

        h_specs = [pl.BlockSpec((pl.Element(tm), pl.Element(d)), x_window), _whole((N_META, d), 2)]
        h_args = list(h)
    else:
        h_specs = [pl.BlockSpec((tm, d), lambda i, j: (i, 0))]
        h_args = [h]
    w_chunk = pl.BlockSpec((d, tn), lambda i, j: (0, j))
    out, tails = pl.pallas_call(
        functools.partial(_conv_prompt_kernel, tiles_per_seq=tiles_per_seq, from_input=from_input),
        grid=(n_tiles, n_chunks),
        in_specs=[
            *h_specs,
            _whole((1, d), 2),
            w_chunk, w_chunk, w_chunk, w_chunk,
            pl.BlockSpec((None, CONV_WIDTH, tn), lambda i, j: (layer, 0, j)),
            pl.BlockSpec((tn, d), lambda i, j: (j, 0)),
        ],
        out_specs=[
            pl.BlockSpec((tm, d), lambda i, j: (i, 0)),
            pl.BlockSpec((1, V7X_SUBLANES, tn), lambda i, j: (i, 0, j)),
        ],
        out_shape=[
            jax.ShapeDtypeStruct((n_rows, d), F32),
            jax.ShapeDtypeStruct((n_tiles, V7X_SUBLANES, d), F32),
        ],
        scratch_shapes=[
            pltpu.VMEM((tm, d), BF16),
            pltpu.VMEM((n_chunks, V7X_SUBLANES, tn), F32),
        ],
        compiler_params=_params("arbitrary", "arbitrary"),
        name="conv_mixer_prompt",
    )(*h_args, norm_w, *w_in_groups, conv_w, w_out)
    state = tails.reshape(batch, tiles_per_seq, V7X_SUBLANES, d)[:, -1, V7X_SUBLANES - (CONV_WIDTH - 1):]
    return out, state


def _conv_layer_sample(h, norm_w, w_in, conv_w, w_out, layer, state, seq_len):
    n_rows, d = h.shape
    tn = COL_TILE
    n_chunks = d // tn
    n_seq = n_rows // seq_len
    halo0 = jnp.repeat(state[:, 0], seq_len, axis=0)
    halo1 = jnp.repeat(state[:, 1], seq_len, axis=0)
    rows_chunk = pl.BlockSpec((n_rows, tn), lambda j: (0, j))
    w_chunk = pl.BlockSpec((d, tn), lambda j: (0, j))
    w_bf16 = jax.ShapeDtypeStruct((d, d), BF16)
    out, cu, *weights = pl.pallas_call(
        functools.partial(_conv_sample_kernel, seg_rows=seq_len),
        grid=(n_chunks,),
        in_specs=[
            _whole((n_rows, d), 1),
            _whole((1, d), 1),
            *_w_in_specs(layer, d, tn, lambda l, c0, j: (l, 0, c0 + j)),
            pl.BlockSpec((None, CONV_WIDTH, tn), lambda j: (layer, 0, j)),
            pl.BlockSpec((None, tn, d), lambda j: (layer, j, 0)),
            rows_chunk, rows_chunk,
        ],
        out_specs=[
            _whole((n_rows, d), 1), rows_chunk,
            w_chunk, w_chunk, w_chunk, w_chunk,
            pl.BlockSpec((tn, d), lambda j: (j, 0)),
        ],
        out_shape=[
            jax.ShapeDtypeStruct((n_rows, d), F32),
            jax.ShapeDtypeStruct((n_rows, d), F32),
            w_bf16, w_bf16, w_bf16, w_bf16, w_bf16,
        ],
        scratch_shapes=[pltpu.VMEM((n_rows, d), BF16)],
        compiler_params=_params("arbitrary"),
        name="conv_mixer_sample",
    )(h, norm_w, w_in, w_in, w_in, w_in, conv_w, w_out, halo0, halo1)
    new_state = cu.reshape(n_seq, seq_len, d)[:, seq_len - (CONV_WIDTH - 1):]
    return out, new_state, weights[:4], weights[4]


def _attn_inproj_sample_kernel(h_ref, nw_ref, wq_ref, wk_ref, wv_ref, wz_ref,
                               q_ref, k_ref, v_ref, z_ref, wq_out, wk_out, wv_out, wz_out, xn_scr):
    @pl.when(pl.program_id(0) == 0)
    def _():
        xn_scr[...] = _rms_norm(h_ref[...], nw_ref[...]).astype(BF16)

    wq, wk, wv, wz = _to_bf16((wq_ref, wk_ref, wv_ref, wz_ref), (wq_out, wk_out, wv_out, wz_out))
    xn = xn_scr[...]
    q_ref[...] = _dot(xn, wq) * SCORE_SCALE
    k_ref[...] = _dot(xn, wk)
    v_ref[...] = _dot(xn, wv)
    z_ref[...] = _dot(xn, wz)


def _attn_inproj_sample(h, norm_w, w_in, layer):
    n_rows, d = h.shape
    tn = COL_TILE
    chunk = pl.BlockSpec((n_rows, tn), lambda j: (0, j))
    w_chunk = pl.BlockSpec((d, tn), lambda j: (0, j))
    outs = pl.pallas_call(
        _attn_inproj_sample_kernel,
        grid=(d // tn,),
        in_specs=[_whole((n_rows, d), 1), _whole((1, d), 1),
                  *_w_in_specs(layer, d, tn, lambda l, c0, j: (l, 0, c0 + j))],
        out_specs=[chunk] * 4 + [w_chunk] * 4,
        out_shape=[jax.ShapeDtypeStruct((n_rows, d), F32)] * 4 + [jax.ShapeDtypeStruct((d, d), BF16)] * 4,
        scratch_shapes=[pltpu.VMEM((n_rows, d), BF16)],
        compiler_params=_params("arbitrary"),
        name="attn_inproj_sample",
    )(h, norm_w, w_in, w_in, w_in, w_in)
    return outs[:4], outs[4:]


def _attn_inproj_prompt_kernel(h_ref, nw_ref, wq_ref, wk_ref, wv_ref, wz_ref, *rest, n_earlier):
    earlier, rest = rest[:2 * n_earlier], rest[2 * n_earlier:]
    q_ref, kb_ref, vb_ref, z_ref, k_out, v_out, xn_scr, chunk_buf, chunk_sem, *staging = rest
    i, j = pl.program_id(0), pl.program_id(1)
    n_tiles, n_chunks = pl.num_programs(0), pl.num_programs(1)
    tm = chunk_buf.shape[2]
    step = i * n_chunks + j
    last_step = step == n_tiles * n_chunks - 1

    @pl.when(j == 0)
    def _():
        xn_scr[...] = _rms_norm(h_ref[...], nw_ref[...]).astype(BF16)

    def chunk_copies(tile, head, slot):
        rows = pl.ds(pl.multiple_of(tile * tm, V7X_SUBLANES), tm)
        for c, out in enumerate((k_out, v_out)):
            dst = out.at[n_earlier, rows, head, :] if n_earlier else out.at[rows, head, :]
            yield pltpu.make_async_copy(chunk_buf.at[c, slot], dst, chunk_sem.at[c, slot])

    slot = lax.rem(step, 2)

    @pl.when(step >= 2)
    def _():
        before = step - 2
        for copy in chunk_copies(before // n_chunks, lax.rem(before, n_chunks), slot):
            copy.wait()

    xn = xn_scr[...]
    q_ref[...] = (_dot(xn, wq_ref[...]) * SCORE_SCALE).astype(BF16)
    z_ref[...] = _dot(xn, wz_ref[...]).astype(BF16)
    k = _dot(xn, wk_ref[...])
    v = _dot(xn, wv_ref[...])
    kb_ref[...] = k.astype(BF16)
    vb_ref[...] = v.astype(BF16)
    chunk_buf[0, slot] = k
    chunk_buf[1, slot] = v
    for copy in chunk_copies(i, j, slot):
        copy.start()

    @pl.when(last_step)
    def _():
        for copy in chunk_copies(i, j, slot):
            copy.wait()
        before = step - 1
        for copy in chunk_copies(before // n_chunks, lax.rem(before, n_chunks), 1 - slot):
            copy.wait()

    if not n_earlier:
        return

    stage, in_sem, out_sem = staging
    slab_rows = stage.shape[2]

    def slab_copies(slab, inward):
        rows = pl.ds(slab * slab_rows, slab_rows)
        slab_slot = lax.rem(slab, 2)
        for a, src in enumerate(earlier):
            buf = stage.at[a, slab_slot]
            if inward:
                yield pltpu.make_async_copy(src.at[rows], buf, in_sem.at[a, slab_slot])
            else:
                yield pltpu.make_async_copy(buf, (k_out, v_out)[a % 2].at[a // 2, rows], out_sem.at[a, slab_slot])

    @pl.when(step >= 2)
    def _():
        for copy in slab_copies(step - 2, inward=False):
            copy.wait()

    for copy in slab_copies(step, inward=True):
        copy.start()

    @pl.when(step >= 1)
    def _():
        for copy in slab_copies(step - 1, inward=True):
            copy.wait()
        for copy in slab_copies(step - 1, inward=False):
            copy.start()

    @pl.when(last_step)
    def _():
        for copy in slab_copies(step, inward=True):
            copy.wait()
        for copy in slab_copies(step, inward=False):
            copy.start()
        for copy in slab_copies(step - 1, inward=False):
            copy.wait()
        for copy in slab_copies(step, inward=False):
            copy.wait()


def _attn_inproj_prompt(h, norm_w, weights, tm, earlier_kv=()):
    n_rows, d = h.shape
    tn = COL_TILE
    assert tn == HEAD_WIDTH
    n_heads = d // tn
    n_earlier = len(earlier_kv)
    n_steps = (n_rows // tm) * n_heads
    assert n_steps >= 2 and n_rows % n_steps == 0
    chunk = pl.BlockSpec((tm, tn), lambda i, j: (i, j))
    w_chunk = pl.BlockSpec((d, tn), lambda i, j: (0, j))
    in_hbm = pl.BlockSpec(memory_space=pl.ANY)
    act = jax.ShapeDtypeStruct((n_rows, d), BF16)
    kv_shape = (n_rows, n_heads, tn)
    scratch = [pltpu.VMEM((tm, d), BF16), pltpu.VMEM((2, 2, tm, tn), F32), pltpu.SemaphoreType.DMA((2, 2))]
    if n_earlier:
        kv_shape = (n_earlier + 1, *kv_shape)
        scratch += [
            pltpu.VMEM((2 * n_earlier, 2, n_rows // n_steps, n_heads, tn), F32),
            pltpu.SemaphoreType.DMA((2 * n_earlier, 2)),
            pltpu.SemaphoreType.DMA((2 * n_earlier, 2)),
        ]
    kv = jax.ShapeDtypeStruct(kv_shape, F32)
    return pl.pallas_call(
        functools.partial(_attn_inproj_prompt_kernel, n_earlier=n_earlier),
        grid=(n_rows // tm, n_heads),
        in_specs=[pl.BlockSpec((tm, d), lambda i, j: (i, 0)), _whole((1, d), 2), w_chunk, w_chunk, w_chunk, w_chunk]
        + [in_hbm] * (2 * n_earlier),
        out_specs=[chunk, chunk, chunk, chunk, in_hbm, in_hbm],
        out_shape=[act, act, act, act, kv, kv],
        scratch_shapes=scratch,
        compiler_params=_params("arbitrary", "arbitrary"),
        name="attn_inproj_prompt",
    )(h, norm_w, *weights, *[a for pair in earlier_kv for a in pair])


def _prompt_attn_kernel(rb_ref, q_ref, k_ref, v_ref, z_ref, near_ref, lam_ref, sw_ref,
                        y_ref, kb_scr, vb_scr, *, lam_init):
    head = pl.program_id(1)
    seq_len = q_ref.shape[1]
    padded_len = kb_scr.shape[0]
    tq = near_ref.shape[1]
    kb_scr[0:seq_len] = k_ref[0]
    vb_scr[0:seq_len] = v_ref[0]
    if padded_len > seq_len:
        kb_scr[seq_len:padded_len] = jnp.zeros((padded_len - seq_len, HEAD_WIDTH), BF16)
        vb_scr[seq_len:padded_len] = jnp.zeros((padded_len - seq_len, HEAD_WIDTH), BF16)
    lam = _lambda(lam_ref[...], lam_init)
    far_bias = rb_ref[N_BUCKETS - 1, head] * LOG2E

    for r0 in range(0, seq_len, tq):
        rows = min(tq, seq_len - r0)
        near_lo = max(r0 - MAX_DISTANCE, 0)
        near_hi = r0 + tq
        near_w = near_hi - near_lo
        col0 = near_lo - (r0 - MAX_DISTANCE)
        q = q_ref[0, r0:r0 + rows, :]
        row = lax.broadcasted_iota(jnp.int32, (rows, near_w), 0)
        col = lax.broadcasted_iota(jnp.int32, (rows, near_w), 1)
        visible = col + col0 <= row + MAX_DISTANCE
        near_bias = near_ref[0, 0:rows, col0:col0 + near_w]
        e_near, e_far, denom = [], [], []
        for c in range(2):
            lanes = slice(c * HEAD_DIM, (c + 1) * HEAD_DIM)
            s_near = _dot_nt(q[:, lanes], kb_scr[near_lo:near_hi, lanes])
            s_near = jnp.where(visible, s_near + near_bias, NEG_INF)
            m = jnp.max(s_near, axis=-1, keepdims=True)
            if near_lo > 0:
                s_far = _dot_nt(q[:, lanes], kb_scr[0:near_lo, lanes])
                m = jnp.maximum(m, jnp.max(s_far, axis=-1, keepdims=True) + far_bias)
                e_far.append(jnp.exp2(s_far - (m - far_bias)))
            e_near.append(jnp.exp2(s_near - m))
            total = jnp.sum(e_near[c], axis=-1, keepdims=True)
            if near_lo > 0:
                total = total + jnp.sum(e_far[c], axis=-1, keepdims=True)
            denom.append(total)
        w1 = 1.0 / denom[0]
        w2 = lam / denom[1]
        a_near = e_near[0] * w1 - e_near[1] * w2
        o = _dot(a_near.astype(BF16), vb_scr[near_lo:near_hi, :])
        if near_lo > 0:
            a_far = e_far[0] * w1 - e_far[1] * w2
            o = o + _dot(a_far.astype(BF16), vb_scr[0:near_lo, :])
        z = z_ref[0, r0:r0 + rows, :].astype(F32)
        y_ref[0, r0:r0 + rows, :] = _sub_norm_gate(o, sw_ref[...], lam_init, z).astype(y_ref.dtype)


def _prompt_attention(q, k, v, z, rel_bias, near_table, lam_params, sub_w, lam_init):
    batch, seq_len, d = q.shape
    n_heads = d // HEAD_WIDTH
    tq = near_table.shape[1]
    padded_len = -(-seq_len // tq) * tq
    head_cols = pl.BlockSpec((1, seq_len, HEAD_WIDTH), lambda b, h: (b, 0, h))
    return pl.pallas_call(
        functools.partial(_prompt_attn_kernel, lam_init=lam_init),
        grid=(batch, n_heads),
        in_specs=[
            pl.BlockSpec(memory_space=pltpu.SMEM),
            head_cols, head_cols, head_cols, head_cols,
            pl.BlockSpec((1, tq, tq + MAX_DISTANCE), lambda b, h: (h, 0, 0)),
            _whole((4, HEAD_DIM), 2),
            _whole((1, HEAD_WIDTH), 2),
        ],
        out_specs=head_cols,
        out_shape=jax.ShapeDtypeStruct((batch, seq_len, d), BF16),
        scratch_shapes=[
            pltpu.VMEM((padded_len, HEAD_WIDTH), BF16),
            pltpu.VMEM((padded_len, HEAD_WIDTH), BF16),
        ],
        compiler_params=_params("arbitrary", "arbitrary"),
        name="prompt_attention",
    )(rel_bias, q, k, v, z, near_table, lam_params, sub_w)


def _decode_attn_kernel(pt_ref, q_ref, kn_ref, vn_ref, z_ref, *rest, n_pg, lam_init):
    k_pages, v_pages = rest[:n_pg], rest[n_pg:2 * n_pg]
    (new_ref, tail_ref, far_ref, lam_ref, sw_ref, y_ref,
     s_scr, m_seen_scr, m_used_scr, l_scr, acc_scr) = rest[2 * n_pg:]
    del pt_ref
    step, n_groups = pl.program_id(1), pl.num_programs(1) - 1
    ls, d = z_ref.shape
    n_heads = d // HEAD_WIDTH
    n_q = n_heads * ls
    page_rows = PAGE_SIZE * n_heads
    tile = (V7X_SUBLANES, V7X_LANES)
    q_cols = q_ref[0]
    far_bias, far_bias_or_0, m_start, valid = far_ref[0:8], far_ref[8:16], far_ref[16:24], far_ref[24:32]
    slot = lax.rem(step, 2)

    def per_tile(x):
        return x.reshape(x.shape[0] // V7X_SUBLANES, *tile)

    def lanes_to_rows(stat):
        per_lane = jnp.sum(stat * valid, axis=0, keepdims=True)
        return jnp.broadcast_to(per_lane, tile).T[:, 0:1]

    def fold(probs, values, m_from, m_to):
        alpha = jnp.exp2(m_from - m_to)
        total = alpha * l_scr[...]
        pv = None
        for p, vb in zip(probs, values):
            total = total + jnp.sum(p, axis=0)
            part = lax.dot_general(p.reshape(vb.shape[0], V7X_LANES), vb, (((0,), (0,)), ((), ())),
                                   preferred_element_type=F32)
            pv = part if pv is None else pv + part
        l_scr[...] = total
        acc_scr[...] = lanes_to_rows(alpha) * acc_scr[...] + pv

    @pl.when(step == 0)
    def _():
        m_seen_scr[...] = m_start
        m_used_scr[...] = m_start
        l_scr[...] = jnp.zeros(l_scr.shape, F32)
        acc_scr[...] = jnp.zeros(acc_scr.shape, F32)
        s_scr[1] = jnp.full(s_scr.shape[1:], NEG_INF, F32)

    keys_are_tail = step >= n_groups - 1
    values_are_tail = step == n_groups

    m_use = m_seen_scr[...]
    prev = s_scr.at[1 - slot]
    probs = []
    for r in range(n_pg):
        s = per_tile(prev[r * page_rows:(r + 1) * page_rows])
        if r == n_pg - 1:
            probs.append(jnp.exp2(s + jnp.where(values_are_tail, per_tile(tail_ref[...]), far_bias) - m_use))
        else:
            probs.append(jnp.exp2(s + (far_bias - m_use)))
    fold(probs, [ref[0, 0].reshape(page_rows, HEAD_WIDTH) for ref in v_pages], m_used_scr[...], m_use)
    m_used_scr[...] = m_use

    m_new = m_use
    for r, ref in enumerate(k_pages):
        logits = _dot(ref[0, 0].reshape(page_rows, HEAD_WIDTH), q_cols)
        s_scr[slot, r * page_rows:(r + 1) * page_rows] = logits
        s = per_tile(logits)
        if r == n_pg - 1:
            s = s + jnp.where(keys_are_tail, per_tile(tail_ref[...]), far_bias_or_0)
            m_new = jnp.maximum(m_new, jnp.max(s, axis=0))
        else:
            m_new = jnp.maximum(m_new, jnp.max(s, axis=0) + far_bias_or_0)
    m_seen_scr[...] = m_new

    @pl.when(step == n_groups)
    def _():
        shape = new_ref.shape
        key = _div(lax.broadcasted_iota(jnp.int32, shape, 0), n_heads)
        query = _rem(lax.broadcasted_iota(jnp.int32, shape, 1), ls)
        s = per_tile(_dot(kn_ref[0], q_cols) + jnp.where(key <= query, new_ref[...], NEG_INF))
        m_all = jnp.maximum(m_use, jnp.max(s, axis=0))
        fold([jnp.exp2(s - m_all)], [vn_ref[0]], m_use, m_all)
        lam = _lambda(lam_ref[...], lam_init)
        o = acc_scr[...] / lanes_to_rows(l_scr[...])
        o = o[0:n_q] - lam * o[n_q:2 * n_q]
        o = o * lax.rsqrt(jnp.mean(o * o, axis=-1, keepdims=True) + EPS)
        o = o * sw_ref[...] * (1.0 - lam_init)
        for h in range(n_heads):
            cols = slice(h * HEAD_WIDTH, (h + 1) * HEAD_WIDTH)
            y_ref[:, cols] = o[h * ls:(h + 1) * ls] * _silu(z_ref[:, cols])


def _decode_attention(q, k_new, v_new, z, cache_k, cache_v, page_table, layer, new_table, tail_table, far_table,
                      lam_params, sub_w, lam_init, ls):
    n_rows, d = q.shape
    batch, n_pages = page_table.shape
    n_heads = d // HEAD_WIDTH
    n_q = n_heads * ls
    new_rows = new_table.shape[0]
    n_pg = math.gcd(n_pages, PAGES_PER_STEP)
    n_groups = n_pages // n_pg
    q_t = q.reshape(batch, ls, n_heads, 2, HEAD_DIM).transpose(0, 3, 4, 2, 1).reshape(batch, 2, HEAD_DIM, n_q)
    q_cols = jnp.zeros((batch, HEAD_WIDTH, V7X_LANES), F32)
    q_cols = q_cols.at[:, :HEAD_DIM, :n_q].set(q_t[:, 0]).at[:, HEAD_DIM:, n_q:2 * n_q].set(q_t[:, 1])
    pad = ((0, 0), (0, new_rows - n_q), (0, 0))
    k_rows = jnp.pad(k_new.reshape(batch, n_q, HEAD_WIDTH), pad)
    v_rows = jnp.pad(v_new.reshape(batch, n_q, HEAD_WIDTH), pad)

    def page_specs(group_of_step):
        return [
            pl.BlockSpec((1, 1, PAGE_SIZE, n_heads, HEAD_WIDTH), functools.partial(
                lambda r, b, s, pt: (layer, pt[b, group_of_step(s) * n_pg + r], 0, 0, 0), r))
            for r in range(n_pg)
        ]

    new_page = pl.BlockSpec((1, new_rows, HEAD_WIDTH), lambda b, s, pt: (b, 0, 0))
    grid_spec = pltpu.PrefetchScalarGridSpec(
        num_scalar_prefetch=1,
        grid=(batch, n_groups + 1),
        in_specs=[
            pl.BlockSpec((1, HEAD_WIDTH, V7X_LANES), lambda b, s, pt: (b, 0, 0)),
            new_page, new_page,
            pl.BlockSpec((ls, d), lambda b, s, pt: (b, 0)),
            *page_specs(lambda s: jnp.minimum(s, n_groups - 1)), *page_specs(lambda s: jnp.maximum(s - 1, 0)),
            _whole(new_table.shape, 3), _whole(tail_table.shape, 3), _whole(far_table.shape, 3),
            _whole((4, HEAD_DIM), 3), _whole((1, HEAD_WIDTH), 3),
        ],
        out_specs=pl.BlockSpec((ls, d), lambda b, s, pt: (b, 0)),
        scratch_shapes=[
            pltpu.VMEM((2, n_pg * PAGE_SIZE * n_heads, V7X_LANES), F32),
            pltpu.VMEM((V7X_SUBLANES, V7X_LANES), F32),
            pltpu.VMEM((V7X_SUBLANES, V7X_LANES), F32),
            pltpu.VMEM((V7X_SUBLANES, V7X_LANES), F32),
            pltpu.VMEM((V7X_LANES, HEAD_WIDTH), F32),
        ],
    )
    return pl.pallas_call(
        functools.partial(_decode_attn_kernel, n_pg=n_pg, lam_init=lam_init),
        grid_spec=grid_spec,
        out_shape=jax.ShapeDtypeStruct((n_rows, d), F32),
        compiler_params=_params("arbitrary", "arbitrary"),
        name="decode_attention",
    )(page_table, q_cols, k_rows, v_rows, z, *([cache_k] * n_pg), *([cache_v] * n_pg),
      new_table, tail_table, far_table, lam_params, sub_w)


def _outproj_kernel(y_ref, w_ref, h_ref, *rest, final_norm):
    h = h_ref[...] + _dot(y_ref[...], w_ref[...])
    if final_norm:
        fw_ref, o_ref = rest
        o_ref[...] = _rms_norm(h, fw_ref[...])
    else:
        (o_ref,) = rest
        o_ref[...] = h


def _outproj_prompt(y, w_out, h, batch, seq_len, final_w=None):
    n_rows, d = h.shape
    w_spec = _whole((d, d), 1)
    if final_w is None:
        tm = _row_tile(seq_len, ROW_TILE_CAP)
        rows = pl.BlockSpec((tm, d), lambda i: (i, 0))
        return pl.pallas_call(
            functools.partial(_outproj_kernel, final_norm=False),
            grid=(n_rows // tm,),
            in_specs=[rows, w_spec, rows],
            out_specs=rows,
            out_shape=jax.ShapeDtypeStruct((n_rows, d), F32),
            compiler_params=_params("arbitrary"),
            name="attn_outproj",
        )(y, w_out, h)
    out_len = seq_len - N_META
    tm = _row_tile(out_len, ROW_TILE_CAP)
    tiles_per_seq = out_len // tm
    assert seq_len % V7X_BF16_SUBLANES == 0 and N_META % V7X_BF16_SUBLANES == 0

    def rows_window(i):
        row = (i // tiles_per_seq) * seq_len + N_META + (i % tiles_per_seq) * tm
        return pl.multiple_of(row, V7X_BF16_SUBLANES), 0

    window = pl.BlockSpec((pl.Element(tm), pl.Element(d)), rows_window)
    return pl.pallas_call(
        functools.partial(_outproj_kernel, final_norm=True),
        grid=(batch * tiles_per_seq,),
        in_specs=[window, w_spec, window, _whole((1, d), 1)],
        out_specs=pl.BlockSpec((tm, d), lambda i: (i, 0)),
        out_shape=jax.ShapeDtypeStruct((batch * out_len, d), F32),
        compiler_params=_params("arbitrary"),
        name="attn_outproj_final",
    )(y, w_out, h, final_w)


def _outproj_sample_kernel(y_ref, w_ref, h_ref, o_ref, w_out):
    (w,) = _to_bf16((w_ref,), (w_out,))
    o_ref[...] = h_ref[...] + _dot(y_ref[...].astype(BF16), w)


def _outproj_sample(y, w_out, layer, h):
    n_rows, d = h.shape
    tn = OUT_COL_TILE
    cols = pl.BlockSpec((n_rows, tn), lambda j: (0, j))
    return pl.pallas_call(
        _outproj_sample_kernel,
        grid=(d // tn,),
        in_specs=[_whole((n_rows, d), 1), pl.BlockSpec((None, d, tn), lambda j: (layer, 0, j)), cols],
        out_specs=[cols, pl.BlockSpec((d, tn), lambda j: (0, j))],
        out_shape=[jax.ShapeDtypeStruct((n_rows, d), F32), jax.ShapeDtypeStruct((d, d), BF16)],
        compiler_params=_params("arbitrary"),
        name="attn_outproj_sample",
    )(y, w_out, h)


def _final_norm_kernel(h_ref, w_ref, o_ref):
    o_ref[...] = _rms_norm(h_ref[...], w_ref[...])


def _final_norm(h, w):
    return pl.pallas_call(
        _final_norm_kernel, out_shape=jax.ShapeDtypeStruct(h.shape, F32), name="final_norm_sample")(h, w)


def kernel(x_prompt, x_sample, state_conv, cache_k, cache_v, page_table, meta_tokens, rel_bias, norm_w,
           final_norm_w, conv_w_in, conv_w, conv_w_out, attn_w_in, attn_lambda, attn_subln_w, attn_w_out):
    b, seq, d = x_prompt.shape
    db, ls, _ = x_sample.shape
    depth = norm_w.shape[0]
    lp = N_META + seq
    n_heads = d // HEAD_WIDTH
    assert d % HEAD_WIDTH == 0 and d % COL_TILE == 0 and d % OUT_COL_TILE == 0
    assert ls >= CONV_WIDTH - 1 and (db * ls) % V7X_BF16_SUBLANES == 0
    assert cache_k.shape[2:] == (PAGE_SIZE, n_heads, HEAD_WIDTH) and cache_v.shape == cache_k.shape
    assert ls <= PAGE_SIZE
    assert depth % 2 == 0, "the final norm is fused into the last attention mixer's out-projection"

    hp = (x_prompt.reshape(b * seq, d), meta_tokens.astype(x_prompt.dtype))
    hs = x_sample.reshape(db * ls, d)
    tm_p = _row_tile(lp, ROW_TILE_CAP)
    fw = final_norm_w.reshape(1, d)

    near_table, new_table, tail_table, far_table = _bias_tables(rel_bias, Q_TILE, ls)

    conv_p, conv_s, k_s, v_s = [], [], [], []
    kv_p = []
    for i in range(depth):
        nw = norm_w[i].reshape(1, d)
        j = i // 2
        last = i == depth - 1
        if i % 2 == 0:
            hs, st_s, w_in_b, w_out_b = _conv_layer_sample(hs, nw, conv_w_in, conv_w, conv_w_out, j, state_conv[j], ls)
            hp, st_p = _conv_layer_prompt(hp, nw, w_in_b, conv_w, w_out_b, j, b, lp)
            conv_p.append(st_p)
            conv_s.append(st_s)
        else:
            lam_init = 0.8 - 0.6 * math.exp(-0.3 * i)
            sub_w = attn_subln_w[j].reshape(1, HEAD_WIDTH)
            (qs, ks, vs, zs), w_in_b = _attn_inproj_sample(hs, nw, attn_w_in, j)
            ys = _decode_attention(qs, ks, vs, zs, cache_k, cache_v, page_table, j, new_table, tail_table,
                                   far_table, attn_lambda[j], sub_w, lam_init, ls)
            hs, w_out_b = _outproj_sample(ys, attn_w_out, j, hs)
            last_attn = j == depth // 2 - 1
            q, k, v, z, k_out, v_out = _attn_inproj_prompt(hp, nw, w_in_b, tm_p, kv_p if last_attn else ())
            kv_p = [(k_out, v_out)] if last_attn else kv_p + [(k_out, v_out)]
            yp = _prompt_attention(q.reshape(b, lp, d), k.reshape(b, lp, d), v.reshape(b, lp, d),
                                   z.reshape(b, lp, d), rel_bias, near_table, attn_lambda[j], sub_w, lam_init)
            hp = _outproj_prompt(yp.reshape(b * lp, d), w_out_b, hp, b, lp, fw if last else None)
            k_s.append(ks.reshape(db, ls, n_heads, HEAD_WIDTH))
            v_s.append(vs.reshape(db, ls, n_heads, HEAD_WIDTH))

    y_prompt = hp.reshape(b, seq, d)
    y_sample = _final_norm(hs, fw).reshape(db, ls, d)
    k_p, v_p = (a.reshape(depth // 2, b, lp, n_heads, HEAD_WIDTH) for a in kv_p[0])
    return (y_prompt, y_sample, jnp.stack(conv_p), jnp.stack(conv_s), k_p, v_p, jnp.stack(k_s), jnp.stack(v_s))
```

```python
import functools
import math

import jax
import jax.numpy as jnp
from jax import lax
from jax.experimental import pallas as pl
from jax.experimental.pallas import tpu as pltpu

N_META = 16
HEAD_DIM = 128
HEAD_WIDTH = 2 * HEAD_DIM
PAGE_SIZE = 128
N_BUCKETS = 32
MAX_EXACT = N_BUCKETS // 2
MAX_DISTANCE = 128
CONV_WIDTH = 3
EPS = 1e-6
NEG_INF = -1e30
LOG2E = math.log2(math.e)
SCORE_SCALE = HEAD_DIM ** -0.5 * LOG2E

BF16 = jnp.bfloat16
F32 = jnp.float32

V7X_LANES = 128
V7X_SUBLANES = 8
V7X_BF16_SUBLANES = 16
V7X_VMEM_LIMIT_BYTES = 56 * 1024 * 1024

ROW_TILE_CAP = 768
COL_TILE = 256
OUT_COL_TILE = 512
Q_TILE = 512
PAGES_PER_STEP = 8


def _row_tile(n_rows, cap):
    best = None
    for t in range(V7X_BF16_SUBLANES, min(n_rows, cap) + 1, V7X_BF16_SUBLANES):
        if n_rows % t == 0:
            best = t
    return n_rows if best is None else best


def _params(*semantics):
    return pltpu.CompilerParams(dimension_semantics=semantics, vmem_limit_bytes=V7X_VMEM_LIMIT_BYTES)


def _dot(a, b):
    return jnp.dot(a, b, preferred_element_type=F32)


def _dot_nt(a, b):
    return lax.dot_general(a, b, (((1,), (1,)), ((), ())), preferred_element_type=F32)


def _rms_norm(x, w):
    return x * lax.rsqrt(jnp.mean(x * x, axis=-1, keepdims=True) + EPS) * w


def _silu(z):
    return z * (1.0 / (1.0 + jnp.exp(-z)))


def _lambda(lp, lam_init):
    a = jnp.sum(lp[0:1] * lp[1:2], axis=-1, keepdims=True)
    b = jnp.sum(lp[2:3] * lp[3:4], axis=-1, keepdims=True)
    return jnp.exp(a) - jnp.exp(b) + lam_init


def _sub_norm_gate(o, sub_w, lam_init, z):
    o = o * lax.rsqrt(jnp.mean(o * o, axis=-1, keepdims=True) + EPS)
    o = o * sub_w * (1.0 - lam_init)
    return o * _silu(z)


def _div(x, n):
    return x >> (n.bit_length() - 1) if n & (n - 1) == 0 else lax.div(x, jnp.int32(n))


def _rem(x, n):
    return x & (n - 1) if n & (n - 1) == 0 else lax.rem(x, jnp.int32(n))


def _to_bf16(src_refs, dst_refs):
    out = []
    for src, dst in zip(src_refs, dst_refs):
        w = src[...].astype(BF16)
        dst[...] = w
        out.append(w)
    return out


def _whole(shape, n_grid_axes):
    return pl.BlockSpec(shape, lambda *_: (0,) * len(shape))


def _bias_tables_kernel(rb_ref, rbl_ref, near_ref, new_ref, tail_ref, far_ref, *, n_heads, ls):
    head = pl.program_id(0)

    def bucket_of(n):
        n = jnp.maximum(n, 0)
        nf = jnp.maximum(n, 1).astype(F32)
        large = MAX_EXACT + (
            jnp.log(nf / MAX_EXACT) / math.log(MAX_DISTANCE / MAX_EXACT) * (N_BUCKETS - MAX_EXACT)
        ).astype(jnp.int32)
        large = jnp.minimum(large, N_BUCKETS - 1)
        return jnp.where(n < MAX_EXACT, n, large)

    side = MAX_DISTANCE
    delta = lax.broadcasted_iota(jnp.int32, (side, side), 0) - lax.broadcasted_iota(jnp.int32, (side, side), 1)
    patterns = {}
    for diagonal in (0, 1):
        bucket = bucket_of(delta + diagonal * side)
        block = jnp.zeros((side, side), F32)
        for b in range(N_BUCKETS):
            block = jnp.where(bucket == b, rb_ref[b, head], block)
        patterns[diagonal] = block * LOG2E
    far = jnp.full((side, side), rb_ref[N_BUCKETS - 1, head] * LOG2E, F32)
    for bi in range(near_ref.shape[1] // side):
        for bj in range(near_ref.shape[2] // side):
            near_ref[0, bi * side:(bi + 1) * side, bj * side:(bj + 1) * side] = patterns.get(bi - bj + 1, far)

    @pl.when(head == 0)
    def _():
        n_q = n_heads * ls

        def slots(shape):
            row = lax.broadcasted_iota(jnp.int32, shape, 0)
            lane = lax.broadcasted_iota(jnp.int32, shape, 1)
            valid = jnp.logical_and(_rem(row, n_heads) == _div(_rem(lane, n_q), ls), lane < 2 * n_q)
            return _div(row, n_heads), _rem(lane, ls), valid

        def paged(ref, offset):
            key, query, valid = slots(ref.shape)
            bucket = bucket_of(query + offset - key)
            bias = jnp.zeros(ref.shape, F32)
            for b in range(N_BUCKETS):
                bias = jnp.where(bucket == b, rbl_ref[b:b + 1, :], bias)
            ref[...] = jnp.where(valid, bias * LOG2E, NEG_INF)

        paged(new_ref, 0)
        paged(tail_ref, PAGE_SIZE)
        tile = (V7X_SUBLANES, V7X_LANES)
        _, _, valid = slots(tile)
        far = jnp.broadcast_to(rbl_ref[N_BUCKETS - 1:N_BUCKETS, :] * LOG2E, tile)
        far_ref[0:8] = jnp.where(valid, far, NEG_INF)
        far_ref[8:16] = jnp.where(valid, far, 0.0)
        far_ref[16:24] = jnp.where(valid, NEG_INF, 0.0)
        far_ref[24:32] = jnp.where(valid, 1.0, 0.0)


def _bias_tables(rel_bias, tq, ls):
    n_heads = rel_bias.shape[1]
    n_q = n_heads * ls
    assert n_heads == V7X_SUBLANES and 2 * n_q <= V7X_LANES
    near_w = tq + MAX_DISTANCE
    page_rows = PAGE_SIZE * n_heads
    new_rows = -(-n_q // V7X_LANES) * V7X_LANES
    lane_head = (jnp.arange(V7X_LANES) % n_q) // ls
    rb_lanes = jnp.where(jnp.arange(V7X_LANES) < 2 * n_q, rel_bias[:, lane_head], 0.0)
    return pl.pallas_call(
        functools.partial(_bias_tables_kernel, n_heads=n_heads, ls=ls),
        grid=(n_heads,),
        in_specs=[pl.BlockSpec(memory_space=pltpu.SMEM), _whole((N_BUCKETS, V7X_LANES), 1)],
        out_specs=[
            pl.BlockSpec((1, tq, near_w), lambda h: (h, 0, 0)),
            _whole((new_rows, V7X_LANES), 1), _whole((page_rows, V7X_LANES), 1),
            _whole((4 * V7X_SUBLANES, V7X_LANES), 1),
        ],
        out_shape=[
            jax.ShapeDtypeStruct((n_heads, tq, near_w), F32),
            jax.ShapeDtypeStruct((new_rows, V7X_LANES), F32),
            jax.ShapeDtypeStruct((page_rows, V7X_LANES), F32),
            jax.ShapeDtypeStruct((4 * V7X_SUBLANES, V7X_LANES), F32),
        ],
        compiler_params=_params("arbitrary"),
        name="bias_tables",
    )(rel_bias, rb_lanes)


def _gated_conv_chunk(xn, wu, wb, wc, wz, cw, t, halo0, halo1):
    u = _dot(xn, wu)
    gate_b = _dot(xn, wb)
    gate_c = _dot(xn, wc)
    z = _dot(xn, wz)
    cu = gate_c * u
    prev1 = jnp.where(t >= 1, pltpu.roll(cu, 1, 0), halo1)
    prev2 = jnp.where(t >= 2, pltpu.roll(cu, 2, 0), jnp.where(t == 1, halo1, halo0))
    conv = prev2 * cw[0:1] + prev1 * cw[1:2] + cu * cw[2:3]
    return cu, gate_b * conv * _silu(z)


def _conv_prompt_kernel(*refs, tiles_per_seq, from_input):
    if from_input:
        x_ref, meta_ref, *refs = refs
    else:
        h_ref, *refs = refs
    nw_ref, wu_ref, wb_ref, wc_ref, wz_ref, cw_ref, wo_ref, o_ref, tail_ref, xn_scr, carry_scr = refs
    i, j = pl.program_id(0), pl.program_id(1)
    tm, tn = o_ref.shape[0], wu_ref.shape[1]
    starts_seq = i % tiles_per_seq == 0

    def start_tile(h):
        xn_scr[...] = _rms_norm(h, nw_ref[...]).astype(BF16)
        o_ref[...] = h

    if from_input:
        @pl.when(jnp.logical_and(j == 0, starts_seq))
        def _():
            start_tile(jnp.concatenate([meta_ref[...], x_ref[0:tm - N_META, :]], axis=0))

        @pl.when(jnp.logical_and(j == 0, jnp.logical_not(starts_seq)))
        def _():
            start_tile(x_ref[...])
    else:
        @pl.when(j == 0)
        def _():
            start_tile(h_ref[...])

    @pl.when(jnp.logical_and(j == 0, starts_seq))
    def _():
        carry_scr[...] = jnp.zeros(carry_scr.shape, F32)

    t = lax.broadcasted_iota(jnp.int32, (tm, tn), 0)
    halo0 = carry_scr[j, V7X_SUBLANES - 2:V7X_SUBLANES - 1, :]
    halo1 = carry_scr[j, V7X_SUBLANES - 1:V7X_SUBLANES, :]
    cu, y = _gated_conv_chunk(xn_scr[...], wu_ref[...], wb_ref[...], wc_ref[...], wz_ref[...], cw_ref[...],
                              t, halo0, halo1)
    last_rows = cu[tm - V7X_SUBLANES:tm]
    carry_scr[j] = last_rows
    tail_ref[0] = last_rows
    o_ref[...] += _dot(y.astype(BF16), wo_ref[...])


def _conv_sample_kernel(h_ref, nw_ref, wu_ref, wb_ref, wc_ref, wz_ref, cw_ref, wo_ref, halo0_ref, halo1_ref,
                        o_ref, cu_ref, wu_out, wb_out, wc_out, wz_out, wo_out, xn_scr, *, seg_rows):
    j = pl.program_id(0)
    tm, tn = h_ref.shape[0], wu_ref.shape[1]

    @pl.when(j == 0)
    def _():
        h = h_ref[...]
        xn_scr[...] = _rms_norm(h, nw_ref[...]).astype(BF16)
        o_ref[...] = h

    wu, wb, wc, wz, wo = _to_bf16((wu_ref, wb_ref, wc_ref, wz_ref, wo_ref), (wu_out, wb_out, wc_out, wz_out, wo_out))
    t = _rem(lax.broadcasted_iota(jnp.int32, (tm, tn), 0), seg_rows)
    cu, y = _gated_conv_chunk(xn_scr[...], wu, wb, wc, wz, cw_ref[...], t, halo0_ref[...], halo1_ref[...])
    cu_ref[...] = cu
    o_ref[...] += _dot(y.astype(BF16), wo)


def _w_in_specs(layer, d, tn, col_map):
    n_chunks = d // tn
    return [
        pl.BlockSpec((None, d, tn), functools.partial(col_map, layer, c * n_chunks)) for c in range(4)
    ]


def _conv_layer_prompt(h, norm_w, w_in_groups, conv_w, w_out, layer, batch, seq_len):
    from_input = isinstance(h, tuple)
    d = w_out.shape[0]
    n_rows = batch * seq_len
    tm, tn = _row_tile(seq_len, ROW_TILE_CAP), COL_TILE
    tiles_per_seq = seq_len // tm
    n_tiles, n_chunks = n_rows // tm, d // tn
    if from_input:
        x_len = seq_len - N_META
        assert tm > N_META and tiles_per_seq * tm - N_META == x_len

        assert x_len % V7X_BF16_SUBLANES == 0 and N_META % V7X_BF16_SUBLANES == 0

        def x_window(i, j):
            row = (i // tiles_per_seq) * x_len + jnp.maximum((i % tiles_per_seq) * tm - N_META, 0)
            return pl.multiple_of(row, V7X_BF16_SUBLANES), 0

        h_specs = [pl.BlockSpec((pl.Element(tm), pl.Element(d)), x_window), _whole((N_META, d), 2)]
        h_args = list(h)
    else:
        h_specs = [pl.BlockSpec((tm, d), lambda i, j: (i, 0))]
        h_args = [h]
    w_chunk = pl.BlockSpec((d, tn), lambda i, j: (0, j))
    out, tails = pl.pallas_call(
        functools.partial(_conv_prompt_kernel, tiles_per_seq=tiles_per_seq, from_input=from_input),
        grid=(n_tiles, n_chunks),
        in_specs=[
            *h_specs,
            _whole((1, d), 2),
            w_chunk, w_chunk, w_chunk, w_chunk,
            pl.BlockSpec((None, CONV_WIDTH, tn), lambda i, j: (layer, 0, j)),
            pl.BlockSpec((tn, d), lambda i, j: (j, 0)),
        ],
        out_specs=[
            pl.BlockSpec((tm, d), lambda i, j: (i, 0)),
            pl.BlockSpec((1, V7X_SUBLANES, tn), lambda i, j: (i, 0, j)),
        ],
        out_shape=[
            jax.ShapeDtypeStruct((n_rows, d), F32),
            jax.ShapeDtypeStruct((n_tiles, V7X_SUBLANES, d), F32),
        ],
        scratch_shapes=[
            pltpu.VMEM((tm, d), BF16),
            pltpu.VMEM((n_chunks, V7X_SUBLANES, tn), F32),
        ],
        compiler_params=_params("arbitrary", "arbitrary"),
        name="conv_mixer_prompt",
    )(*h_args, norm_w, *w_in_groups, conv_w, w_out)
    state = tails.reshape(batch, tiles_per_seq, V7X_SUBLANES, d)[:, -1, V7X_SUBLANES - (CONV_WIDTH - 1):]
    return out, state


def _conv_layer_sample(h, norm_w, w_in, conv_w, w_out, layer, state, seq_len):
    n_rows, d = h.shape
    tn = COL_TILE
    n_chunks = d // tn
    n_seq = n_rows // seq_len
    halo0 = jnp.repeat(state[:, 0], seq_len, axis=0)
    halo1 = jnp.repeat(state[:, 1], seq_len, axis=0)
    rows_chunk = pl.BlockSpec((n_rows, tn), lambda j: (0, j))
    w_chunk = pl.BlockSpec((d, tn), lambda j: (0, j))
    w_bf16 = jax.ShapeDtypeStruct((d, d), BF16)
    out, cu, *weights = pl.pallas_call(
        functools.partial(_conv_sample_kernel, seg_rows=seq_len),
        grid=(n_chunks,),
        in_specs=[
            _whole((n_rows, d), 1),
            _whole((1, d), 1),
            *_w_in_specs(layer, d, tn, lambda l, c0, j: (l, 0, c0 + j)),
            pl.BlockSpec((None, CONV_WIDTH, tn), lambda j: (layer, 0, j)),
            pl.BlockSpec((None, tn, d), lambda j: (layer, j, 0)),
            rows_chunk, rows_chunk,
        ],
        out_specs=[
            _whole((n_rows, d), 1), rows_chunk,
            w_chunk, w_chunk, w_chunk, w_chunk,
            pl.BlockSpec((tn, d), lambda j: (j, 0)),
        ],
        out_shape=[
            jax.ShapeDtypeStruct((n_rows, d), F32),
            jax.ShapeDtypeStruct((n_rows, d), F32),
            w_bf16, w_bf16, w_bf16, w_bf16, w_bf16,
        ],
        scratch_shapes=[pltpu.VMEM((n_rows, d), BF16)],
        compiler_params=_params("arbitrary"),
        name="conv_mixer_sample",
    )(h, norm_w, w_in, w_in, w_in, w_in, conv_w, w_out, halo0, halo1)
    new_state = cu.reshape(n_seq, seq_len, d)[:, seq_len - (CONV_WIDTH - 1):]
    return out, new_state, weights[:4], weights[4]


def _attn_inproj_sample_kernel(h_ref, nw_ref, wq_ref, wk_ref, wv_ref, wz_ref,
                               q_ref, k_ref, v_ref, z_ref, wq_out, wk_out, wv_out, wz_out, xn_scr):
    @pl.when(pl.program_id(0) == 0)
    def _():
        xn_scr[...] = _rms_norm(h_ref[...], nw_ref[...]).astype(BF16)

    wq, wk, wv, wz = _to_bf16((wq_ref, wk_ref, wv_ref, wz_ref), (wq_out, wk_out, wv_out, wz_out))
    xn = xn_scr[...]
    q_ref[...] = _dot(xn, wq) * SCORE_SCALE
    k_ref[...] = _dot(xn, wk)
    v_ref[...] = _dot(xn, wv)
    z_ref[...] = _dot(xn, wz)


def _attn_inproj_sample(h, norm_w, w_in, layer):
    n_rows, d = h.shape
    tn = COL_TILE
    chunk = pl.BlockSpec((n_rows, tn), lambda j: (0, j))
    w_chunk = pl.BlockSpec((d, tn), lambda j: (0, j))
    outs = pl.pallas_call(
        _attn_inproj_sample_kernel,
        grid=(d // tn,),
        in_specs=[_whole((n_rows, d), 1), _whole((1, d), 1),
                  *_w_in_specs(layer, d, tn, lambda l, c0, j: (l, 0, c0 + j))],
        out_specs=[chunk] * 4 + [w_chunk] * 4,
        out_shape=[jax.ShapeDtypeStruct((n_rows, d), F32)] * 4 + [jax.ShapeDtypeStruct((d, d), BF16)] * 4,
        scratch_shapes=[pltpu.VMEM((n_rows, d), BF16)],
        compiler_params=_params("arbitrary"),
        name="attn_inproj_sample",
    )(h, norm_w, w_in, w_in, w_in, w_in)
    return outs[:4], outs[4:]


def _attn_inproj_prompt_kernel(h_ref, nw_ref, wq_ref, wk_ref, wv_ref, wz_ref, *rest, n_earlier):
    earlier, rest = rest[:2 * n_earlier], rest[2 * n_earlier:]
    q_ref, kb_ref, vb_ref, z_ref, k_out, v_out, xn_scr, chunk_buf, chunk_sem, *staging = rest
    i, j = pl.program_id(0), pl.program_id(1)
    n_tiles, n_chunks = pl.num_programs(0), pl.num_programs(1)
    tm = chunk_buf.shape[2]
    step = i * n_chunks + j
    last_step = step == n_tiles * n_chunks - 1

    @pl.when(j == 0)
    def _():
        xn_scr[...] = _rms_norm(h_ref[...], nw_ref[...]).astype(BF16)

    def chunk_copies(tile, head, slot):
        rows = pl.ds(pl.multiple_of(tile * tm, V7X_SUBLANES), tm)
        for c, out in enumerate((k_out, v_out)):
            dst = out.at[n_earlier, rows, head, :] if n_earlier else out.at[rows, head, :]
            yield pltpu.make_async_copy(chunk_buf.at[c, slot], dst, chunk_sem.at[c, slot])

    slot = lax.rem(step, 2)

    @pl.when(step >= 2)
    def _():
        before = step - 2
        for copy in chunk_copies(before // n_chunks, lax.rem(before, n_chunks), slot):
            copy.wait()

    xn = xn_scr[...]
    q_ref[...] = (_dot(xn, wq_ref[...]) * SCORE_SCALE).astype(BF16)
    z_ref[...] = _dot(xn, wz_ref[...]).astype(BF16)
    k = _dot(xn, wk_ref[...])
    v = _dot(xn, wv_ref[...])
    kb_ref[...] = k.astype(BF16)
    vb_ref[...] = v.astype(BF16)
    chunk_buf[0, slot] = k
    chunk_buf[1, slot] = v
    for copy in chunk_copies(i, j, slot):
        copy.start()

    @pl.when(last_step)
    def _():
        for copy in chunk_copies(i, j, slot):
            copy.wait()
        before = step - 1
        for copy in chunk_copies(before // n_chunks, lax.rem(before, n_chunks), 1 - slot):
            copy.wait()

    if not n_earlier:
        return

    stage, in_sem, out_sem = staging
    slab_rows = stage.shape[2]

    def slab_copies(slab, inward):
        rows = pl.ds(slab * slab_rows, slab_rows)
        slab_slot = lax.rem(slab, 2)
        for a, src in enumerate(earlier):
            buf = stage.at[a, slab_slot]
            if inward:
                yield pltpu.make_async_copy(src.at[rows], buf, in_sem.at[a, slab_slot])
            else:
                yield pltpu.make_async_copy(buf, (k_out, v_out)[a % 2].at[a // 2, rows], out_sem.at[a, slab_slot])

    @pl.when(step >= 2)
    def _():
        for copy in slab_copies(step - 2, inward=False):
            copy.wait()

    for copy in slab_copies(step, inward=True):
        copy.start()

    @pl.when(step >= 1)
    def _():
        for copy in slab_copies(step - 1, inward=True):
            copy.wait()
        for copy in slab_copies(step - 1, inward=False):
            copy.start()

    @pl.when(last_step)
    def _():
        for copy in slab_copies(step, inward=True):
            copy.wait()
        for copy in slab_copies(step, inward=False):
            copy.start()
        for copy in slab_copies(step - 1, inward=False):
            copy.wait()
        for copy in slab_copies(step, inward=False):
            copy.wait()


def _attn_inproj_prompt(h, norm_w, weights, tm, earlier_kv=()):
    n_rows, d = h.shape
    tn = COL_TILE
    assert tn == HEAD_WIDTH
    n_heads = d // tn
    n_earlier = len(earlier_kv)
    n_steps = (n_rows // tm) * n_heads
    assert n_steps >= 2 and n_rows % n_steps == 0
    chunk = pl.BlockSpec((tm, tn), lambda i, j: (i, j))
    w_chunk = pl.BlockSpec((d, tn), lambda i, j: (0, j))
    in_hbm = pl.BlockSpec(memory_space=pl.ANY)
    act = jax.ShapeDtypeStruct((n_rows, d), BF16)
    kv_shape = (n_rows, n_heads, tn)
    scratch = [pltpu.VMEM((tm, d), BF16), pltpu.VMEM((2, 2, tm, tn), F32), pltpu.SemaphoreType.DMA((2, 2))]
    if n_earlier:
        kv_shape = (n_earlier + 1, *kv_shape)
        scratch += [
            pltpu.VMEM((2 * n_earlier, 2, n_rows // n_steps, n_heads, tn), F32),
            pltpu.SemaphoreType.DMA((2 * n_earlier, 2)),
            pltpu.SemaphoreType.DMA((2 * n_earlier, 2)),
        ]
    kv = jax.ShapeDtypeStruct(kv_shape, F32)
    return pl.pallas_call(
        functools.partial(_attn_inproj_prompt_kernel, n_earlier=n_earlier),
        grid=(n_rows // tm, n_heads),
        in_specs=[pl.BlockSpec((tm, d), lambda i, j: (i, 0)), _whole((1, d), 2), w_chunk, w_chunk, w_chunk, w_chunk]
        + [in_hbm] * (2 * n_earlier),
        out_specs=[chunk, chunk, chunk, chunk, in_hbm, in_hbm],
        out_shape=[act, act, act, act, kv, kv],
        scratch_shapes=scratch,
        compiler_params=_params("arbitrary", "arbitrary"),
        name="attn_inproj_prompt",
    )(h, norm_w, *weights, *[a for pair in earlier_kv for a in pair])


def _prompt_attn_kernel(rb_ref, q_ref, k_ref, v_ref, z_ref, near_ref, lam_ref, sw_ref,
                        y_ref, kb_scr, vb_scr, *, lam_init):
    head = pl.program_id(1)
    seq_len = q_ref.shape[1]
    padded_len = kb_scr.shape[0]
    tq = near_ref.shape[1]
    kb_scr[0:seq_len] = k_ref[0]
    vb_scr[0:seq_len] = v_ref[0]
    if padded_len > seq_len:
        kb_scr[seq_len:padded_len] = jnp.zeros((padded_len - seq_len, HEAD_WIDTH), BF16)
        vb_scr[seq_len:padded_len] = jnp.zeros((padded_len - seq_len, HEAD_WIDTH), BF16)
    lam = _lambda(lam_ref[...], lam_init)
    far_bias = rb_ref[N_BUCKETS - 1, head] * LOG2E

    for r0 in range(0, seq_len, tq):
        rows = min(tq, seq_len - r0)
        near_lo = max(r0 - MAX_DISTANCE, 0)
        near_hi = r0 + tq
        near_w = near_hi - near_lo
        col0 = near_lo - (r0 - MAX_DISTANCE)
        q = q_ref[0, r0:r0 + rows, :]
        row = lax.broadcasted_iota(jnp.int32, (rows, near_w), 0)
        col = lax.broadcasted_iota(jnp.int32, (rows, near_w), 1)
        visible = col + col0 <= row + MAX_DISTANCE
        near_bias = near_ref[0, 0:rows, col0:col0 + near_w]
        e_near, e_far, denom = [], [], []
        for c in range(2):
            lanes = slice(c * HEAD_DIM, (c + 1) * HEAD_DIM)
            s_near = _dot_nt(q[:, lanes], kb_scr[near_lo:near_hi, lanes])
            s_near = jnp.where(visible, s_near + near_bias, NEG_INF)
            m = jnp.max(s_near, axis=-1, keepdims=True)
            if near_lo > 0:
                s_far = _dot_nt(q[:, lanes], kb_scr[0:near_lo, lanes])
                m = jnp.maximum(m, jnp.max(s_far, axis=-1, keepdims=True) + far_bias)
                e_far.append(jnp.exp2(s_far - (m - far_bias)))
            e_near.append(jnp.exp2(s_near - m))
            total = jnp.sum(e_near[c], axis=-1, keepdims=True)
            if near_lo > 0:
                total = total + jnp.sum(e_far[c], axis=-1, keepdims=True)
            denom.append(total)
        w1 = 1.0 / denom[0]
        w2 = lam / denom[1]
        a_near = e_near[0] * w1 - e_near[1] * w2
        o = _dot(a_near.astype(BF16), vb_scr[near_lo:near_hi, :])
        if near_lo > 0:
            a_far = e_far[0] * w1 - e_far[1] * w2
            o = o + _dot(a_far.astype(BF16), vb_scr[0:near_lo, :])
        z = z_ref[0, r0:r0 + rows, :].astype(F32)
        y_ref[0, r0:r0 + rows, :] = _sub_norm_gate(o, sw_ref[...], lam_init, z).astype(y_ref.dtype)


def _prompt_attention(q, k, v, z, rel_bias, near_table, lam_params, sub_w, lam_init):
    batch, seq_len, d = q.shape
    n_heads = d // HEAD_WIDTH
    tq = near_table.shape[1]
    padded_len = -(-seq_len // tq) * tq
    head_cols = pl.BlockSpec((1, seq_len, HEAD_WIDTH), lambda b, h: (b, 0, h))
    return pl.pallas_call(
        functools.partial(_prompt_attn_kernel, lam_init=lam_init),
        grid=(batch, n_heads),
        in_specs=[
            pl.BlockSpec(memory_space=pltpu.SMEM),
            head_cols, head_cols, head_cols, head_cols,
            pl.BlockSpec((1, tq, tq + MAX_DISTANCE), lambda b, h: (h, 0, 0)),
            _whole((4, HEAD_DIM), 2),
            _whole((1, HEAD_WIDTH), 2),
        ],
        out_specs=head_cols,
        out_shape=jax.ShapeDtypeStruct((batch, seq_len, d), BF16),
        scratch_shapes=[
            pltpu.VMEM((padded_len, HEAD_WIDTH), BF16),
            pltpu.VMEM((padded_len, HEAD_WIDTH), BF16),
        ],
        compiler_params=_params("arbitrary", "arbitrary"),
        name="prompt_attention",
    )(rel_bias, q, k, v, z, near_table, lam_params, sub_w)


def _decode_attn_kernel(pt_ref, q_ref, kn_ref, vn_ref, z_ref, *rest, n_pg, lam_init):
    k_pages, v_pages = rest[:n_pg], rest[n_pg:2 * n_pg]
    (new_ref, tail_ref, far_ref, lam_ref, sw_ref, y_ref,
     s_scr, m_seen_scr, m_used_scr, l_scr, acc_scr) = rest[2 * n_pg:]
    del pt_ref
    step, n_groups = pl.program_id(1), pl.num_programs(1) - 1
    ls, d = z_ref.shape
    n_heads = d // HEAD_WIDTH
    n_q = n_heads * ls
    page_rows = PAGE_SIZE * n_heads
    tile = (V7X_SUBLANES, V7X_LANES)
    q_cols = q_ref[0]
    far_bias, far_bias_or_0, m_start, valid = far_ref[0:8], far_ref[8:16], far_ref[16:24], far_ref[24:32]
    slot = lax.rem(step, 2)

    def per_tile(x):
        return x.reshape(x.shape[0] // V7X_SUBLANES, *tile)

    def lanes_to_rows(stat):
        per_lane = jnp.sum(stat * valid, axis=0, keepdims=True)
        return jnp.broadcast_to(per_lane, tile).T[:, 0:1]

    def fold(probs, values, m_from, m_to):
        alpha = jnp.exp2(m_from - m_to)
        total = alpha * l_scr[...]
        pv = None
        for p, vb in zip(probs, values):
            total = total + jnp.sum(p, axis=0)
            part = lax.dot_general(p.reshape(vb.shape[0], V7X_LANES), vb, (((0,), (0,)), ((), ())),
                                   preferred_element_type=F32)
            pv = part if pv is None else pv + part
        l_scr[...] = total
        acc_scr[...] = lanes_to_rows(alpha) * acc_scr[...] + pv

    @pl.when(step == 0)
    def _():
        m_seen_scr[...] = m_start
        m_used_scr[...] = m_start
        l_scr[...] = jnp.zeros(l_scr.shape, F32)
        acc_scr[...] = jnp.zeros(acc_scr.shape, F32)
        s_scr[1] = jnp.full(s_scr.shape[1:], NEG_INF, F32)

    keys_are_tail = step >= n_groups - 1
    values_are_tail = step == n_groups

    m_use = m_seen_scr[...]
    prev = s_scr.at[1 - slot]
    probs = []
    for r in range(n_pg):
        s = per_tile(prev[r * page_rows:(r + 1) * page_rows])
        if r == n_pg - 1:
            probs.append(jnp.exp2(s + jnp.where(values_are_tail, per_tile(tail_ref[...]), far_bias) - m_use))
        else:
            probs.append(jnp.exp2(s + (far_bias - m_use)))
    fold(probs, [ref[0, 0].reshape(page_rows, HEAD_WIDTH) for ref in v_pages], m_used_scr[...], m_use)
    m_used_scr[...] = m_use

    m_new = m_use
    for r, ref in enumerate(k_pages):
        logits = _dot(ref[0, 0].reshape(page_rows, HEAD_WIDTH), q_cols)
        s_scr[slot, r * page_rows:(r + 1) * page_rows] = logits
        s = per_tile(logits)
        if r == n_pg - 1:
            s = s + jnp.where(keys_are_tail, per_tile(tail_ref[...]), far_bias_or_0)
            m_new = jnp.maximum(m_new, jnp.max(s, axis=0))
        else:
            m_new = jnp.maximum(m_new, jnp.max(s, axis=0) + far_bias_or_0)
    m_seen_scr[...] = m_new

    @pl.when(step == n_groups)
    def _():
        shape = new_ref.shape
        key = _div(lax.broadcasted_iota(jnp.int32, shape, 0), n_heads)
        query = _rem(lax.broadcasted_iota(jnp.int32, shape, 1), ls)
        s = per_tile(_dot(kn_ref[0], q_cols) + jnp.where(key <= query, new_ref[...], NEG_INF))
        m_all = jnp.maximum(m_use, jnp.max(s, axis=0))
        fold([jnp.exp2(s - m_all)], [vn_ref[0]], m_use, m_all)
        lam = _lambda(lam_ref[...], lam_init)
        o = acc_scr[...] / lanes_to_rows(l_scr[...])
        o = o[0:n_q] - lam * o[n_q:2 * n_q]
        o = o * lax.rsqrt(jnp.mean(o * o, axis=-1, keepdims=True) + EPS)
        o = o * sw_ref[...] * (1.0 - lam_init)
        for h in range(n_heads):
            cols = slice(h * HEAD_WIDTH, (h + 1) * HEAD_WIDTH)
            y_ref[:, cols] = o[h * ls:(h + 1) * ls] * _silu(z_ref[:, cols])


def _decode_attention(q, k_new, v_new, z, cache_k, cache_v, page_table, layer, new_table, tail_table, far_table,
                      lam_params, sub_w, lam_init, ls):
    n_rows, d = q.shape
    batch, n_pages = page_table.shape
    n_heads = d // HEAD_WIDTH
    n_q = n_heads * ls
    new_rows = new_table.shape[0]
    n_pg = math.gcd(n_pages, PAGES_PER_STEP)
    n_groups = n_pages // n_pg
    q_t = q.reshape(batch, ls, n_heads, 2, HEAD_DIM).transpose(0, 3, 4, 2, 1).reshape(batch, 2, HEAD_DIM, n_q)
    q_cols = jnp.zeros((batch, HEAD_WIDTH, V7X_LANES), F32)
    q_cols = q_cols.at[:, :HEAD_DIM, :n_q].set(q_t[:, 0]).at[:, HEAD_DIM:, n_q:2 * n_q].set(q_t[:, 1])
    pad = ((0, 0), (0, new_rows - n_q), (0, 0))
    k_rows = jnp.pad(k_new.reshape(batch, n_q, HEAD_WIDTH), pad)
    v_rows = jnp.pad(v_new.reshape(batch, n_q, HEAD_WIDTH), pad)

    def page_specs(group_of_step):
        return [
            pl.BlockSpec((1, 1, PAGE_SIZE, n_heads, HEAD_WIDTH), functools.partial(
                lambda r, b, s, pt: (layer, pt[b, group_of_step(s) * n_pg + r], 0, 0, 0), r))
            for r in range(n_pg)
        ]

    new_page = pl.BlockSpec((1, new_rows, HEAD_WIDTH), lambda b, s, pt: (b, 0, 0))
    grid_spec = pltpu.PrefetchScalarGridSpec(
        num_scalar_prefetch=1,
        grid=(batch, n_groups + 1),
        in_specs=[
            pl.BlockSpec((1, HEAD_WIDTH, V7X_LANES), lambda b, s, pt: (b, 0, 0)),
            new_page, new_page,
            pl.BlockSpec((ls, d), lambda b, s, pt: (b, 0)),
            *page_specs(lambda s: jnp.minimum(s, n_groups - 1)), *page_specs(lambda s: jnp.maximum(s - 1, 0)),
            _whole(new_table.shape, 3), _whole(tail_table.shape, 3), _whole(far_table.shape, 3),
            _whole((4, HEAD_DIM), 3), _whole((1, HEAD_WIDTH), 3),
        ],
        out_specs=pl.BlockSpec((ls, d), lambda b, s, pt: (b, 0)),
        scratch_shapes=[
            pltpu.VMEM((2, n_pg * PAGE_SIZE * n_heads, V7X_LANES), F32),
            pltpu.VMEM((V7X_SUBLANES, V7X_LANES), F32),
            pltpu.VMEM((V7X_SUBLANES, V7X_LANES), F32),
            pltpu.VMEM((V7X_SUBLANES, V7X_LANES), F32),
            pltpu.VMEM((V7X_LANES, HEAD_WIDTH), F32),
        ],
    )
    return pl.pallas_call(
        functools.partial(_decode_attn_kernel, n_pg=n_pg, lam_init=lam_init),
        grid_spec=grid_spec,
        out_shape=jax.ShapeDtypeStruct((n_rows, d), F32),
        compiler_params=_params("arbitrary", "arbitrary"),
        name="decode_attention",
    )(page_table, q_cols, k_rows, v_rows, z, *([cache_k] * n_pg), *([cache_v] * n_pg),
      new_table, tail_table, far_table, lam_params, sub_w)


def _outproj_kernel(y_ref, w_ref, h_ref, *rest, final_norm):
    h = h_ref[...] + _dot(y_ref[...], w_ref[...])
    if final_norm:
        fw_ref, o_ref = rest
        o_ref[...] = _rms_norm(h, fw_ref[...])
    else:
        (o_ref,) = rest
        o_ref[...] = h


def _outproj_prompt(y, w_out, h, batch, seq_len, final_w=None):
    n_rows, d = h.shape
    w_spec = _whole((d, d), 1)
    if final_w is None:
        tm = _row_tile(seq_len, ROW_TILE_CAP)
        rows = pl.BlockSpec((tm, d), lambda i: (i, 0))
        return pl.pallas_call(
            functools.partial(_outproj_kernel, final_norm=False),
            grid=(n_rows // tm,),
            in_specs=[rows, w_spec, rows],
            out_specs=rows,
            out_shape=jax.ShapeDtypeStruct((n_rows, d), F32),
            compiler_params=_params("arbitrary"),
            name="attn_outproj",
        )(y, w_out, h)
    out_len = seq_len - N_META
    tm = _row_tile(out_len, ROW_TILE_CAP)
    tiles_per_seq = out_len // tm
    assert seq_len % V7X_BF16_SUBLANES == 0 and N_META % V7X_BF16_SUBLANES == 0

    def rows_window(i):
        row = (i // tiles_per_seq) * seq_len + N_META + (i % tiles_per_seq) * tm
        return pl.multiple_of(row, V7X_BF16_SUBLANES), 0

    window = pl.BlockSpec((pl.Element(tm), pl.Element(d)), rows_window)
    return pl.pallas_call(
        functools.partial(_outproj_kernel, final_norm=True),
        grid=(batch * tiles_per_seq,),
        in_specs=[window, w_spec, window, _whole((1, d), 1)],
        out_specs=pl.BlockSpec((tm, d), lambda i: (i, 0)),
        out_shape=jax.ShapeDtypeStruct((batch * out_len, d), F32),
        compiler_params=_params("arbitrary"),
        name="attn_outproj_final",
    )(y, w_out, h, final_w)


def _outproj_sample_kernel(y_ref, w_ref, h_ref, o_ref, w_out):
    (w,) = _to_bf16((w_ref,), (w_out,))
    o_ref[...] = h_ref[...] + _dot(y_ref[...].astype(BF16), w)


def _outproj_sample(y, w_out, layer, h):
    n_rows, d = h.shape
    tn = OUT_COL_TILE
    cols = pl.BlockSpec((n_rows, tn), lambda j: (0, j))
    return pl.pallas_call(
        _outproj_sample_kernel,
        grid=(d // tn,),
        in_specs=[_whole((n_rows, d), 1), pl.BlockSpec((None, d, tn), lambda j: (layer, 0, j)), cols],
        out_specs=[cols, pl.BlockSpec((d, tn), lambda j: (0, j))],
        out_shape=[jax.ShapeDtypeStruct((n_rows, d), F32), jax.ShapeDtypeStruct((d, d), BF16)],
        compiler_params=_params("arbitrary"),
        name="attn_outproj_sample",
    )(y, w_out, h)


def _final_norm_kernel(h_ref, w_ref, o_ref):
    o_ref[...] = _rms_norm(h_ref[...], w_ref[...])


def _final_norm(h, w):
    return pl.pallas_call(
        _final_norm_kernel, out_shape=jax.ShapeDtypeStruct(h.shape, F32), name="final_norm_sample")(h, w)


def kernel(x_prompt, x_sample, state_conv, cache_k, cache_v, page_table, meta_tokens, rel_bias, norm_w,
           final_norm_w, conv_w_in, conv_w, conv_w_out, attn_w_in, attn_lambda, attn_subln_w, attn_w_out):
    b, seq, d = x_prompt.shape
    db, ls, _ = x_sample.shape
    depth = norm_w.shape[0]
    lp = N_META + seq
    n_heads = d // HEAD_WIDTH
    assert d % HEAD_WIDTH == 0 and d % COL_TILE == 0 and d % OUT_COL_TILE == 0
    assert ls >= CONV_WIDTH - 1 and (db * ls) % V7X_BF16_SUBLANES == 0
    assert cache_k.shape[2:] == (PAGE_SIZE, n_heads, HEAD_WIDTH) and cache_v.shape == cache_k.shape
    assert ls <= PAGE_SIZE
    assert depth % 2 == 0, "the final norm is fused into the last attention mixer's out-projection"

    hp = (x_prompt.reshape(b * seq, d), meta_tokens.astype(x_prompt.dtype))
    hs = x_sample.reshape(db * ls, d)
    tm_p = _row_tile(lp, ROW_TILE_CAP)
    fw = final_norm_w.reshape(1, d)

    near_table, new_table, tail_table, far_table = _bias_tables(rel_bias, Q_TILE, ls)

    conv_p, conv_s, k_s, v_s = [], [], [], []
    kv_p = []
    for i in range(depth):
        nw = norm_w[i].reshape(1, d)
        j = i // 2
        last = i == depth - 1
        if i % 2 == 0:
            hs, st_s, w_in_b, w_out_b = _conv_layer_sample(hs, nw, conv_w_in, conv_w, conv_w_out, j, state_conv[j], ls)
            hp, st_p = _conv_layer_prompt(hp, nw, w_in_b, conv_w, w_out_b, j, b, lp)
            conv_p.append(st_p)
            conv_s.append(st_s)
        else:
            lam_init = 0.8 - 0.6 * math.exp(-0.3 * i)
            sub_w = attn_subln_w[j].reshape(1, HEAD_WIDTH)
            (qs, ks, vs, zs), w_in_b = _attn_inproj_sample(hs, nw, attn_w_in, j)
            ys = _decode_attention(qs, ks, vs, zs, cache_k, cache_v, page_table, j, new_table, tail_table,
                                   far_table, attn_lambda[j], sub_w, lam_init, ls)
            hs, w_out_b = _outproj_sample(ys, attn_w_out, j, hs)
            last_attn = j == depth // 2 - 1
            q, k, v, z, k_out, v_out = _attn_inproj_prompt(hp, nw, w_in_b, tm_p, kv_p if last_attn else ())
            kv_p = [(k_out, v_out)] if last_attn else kv_p + [(k_out, v_out)]
            yp = _prompt_attention(q.reshape(b, lp, d), k.reshape(b, lp, d), v.reshape(b, lp, d),
                                   z.reshape(b, lp, d), rel_bias, near_table, attn_lambda[j], sub_w, lam_init)
            hp = _outproj_prompt(yp.reshape(b * lp, d), w_out_b, hp, b, lp, fw if last else None)
            k_s.append(ks.reshape(db, ls, n_heads, HEAD_WIDTH))
            v_s.append(vs.reshape(db, ls, n_heads, HEAD_WIDTH))

    y_prompt = hp.reshape(b, seq, d)
    y_sample = _final_norm(hs, fw).reshape(db, ls, d)
    k_p, v_p = (a.reshape(depth // 2, b, lp, n_heads, HEAD_WIDTH) for a in kv_p[0])
    return (y_prompt, y_sample, jnp.stack(conv_p), jnp.stack(conv_s), k_p, v_p, jnp.stack(k_s), jnp.stack(v_s))
```

```python
import functools
import math

import jax
import jax.numpy as jnp
from jax import lax
from jax.experimental import pallas as pl
from jax.experimental.pallas import tpu as pltpu

N_META = 16
HEAD_DIM = 128
HEAD_WIDTH = 2 * HEAD_DIM
PAGE_SIZE = 128
N_BUCKETS = 32
MAX_EXACT = N_BUCKETS // 2
MAX_DISTANCE = 128
CONV_WIDTH = 3
EPS = 1e-6
NEG_INF = -1e30
LOG2E = math.log2(math.e)
SCORE_SCALE = HEAD_DIM ** -0.5 * LOG2E

BF16 = jnp.bfloat16
F32 = jnp.float32

V7X_LANES = 128
V7X_SUBLANES = 8
V7X_BF16_SUBLANES = 16
V7X_VMEM_LIMIT_BYTES = 56 * 1024 * 1024

ROW_TILE_CAP = 768
COL_TILE = 256
PROMPT_COL_TILE = 512
OUT_COL_TILE = 512
Q_TILE = 512
PAGES_PER_STEP = 8


def _row_tile(n_rows, cap):
    best = None
    for t in range(V7X_BF16_SUBLANES, min(n_rows, cap) + 1, V7X_BF16_SUBLANES):
        if n_rows % t == 0:
            best = t
    return n_rows if best is None else best


def _params(*semantics):
    return pltpu.CompilerParams(dimension_semantics=semantics, vmem_limit_bytes=V7X_VMEM_LIMIT_BYTES)


def _dot(a, b):
    return jnp.dot(a, b, preferred_element_type=F32)


def _dot_nt(a, b):
    return lax.dot_general(a, b, (((1,), (1,)), ((), ())), preferred_element_type=F32)


def _rms_norm(x, w):
    return x * lax.rsqrt(jnp.mean(x * x, axis=-1, keepdims=True) + EPS) * w


def _silu(z):
    return z * (1.0 / (1.0 + jnp.exp(-z)))


def _lambda(lp, lam_init):
    a = jnp.sum(lp[0:1] * lp[1:2], axis=-1, keepdims=True)
    b = jnp.sum(lp[2:3] * lp[3:4], axis=-1, keepdims=True)
    return jnp.exp(a) - jnp.exp(b) + lam_init


def _sub_norm_gate(o, sub_w, lam_init, z):
    o = o * lax.rsqrt(jnp.mean(o * o, axis=-1, keepdims=True) + EPS)
    o = o * sub_w * (1.0 - lam_init)
    return o * _silu(z)


def _div(x, n):
    return x >> (n.bit_length() - 1) if n & (n - 1) == 0 else lax.div(x, jnp.int32(n))


def _rem(x, n):
    return x & (n - 1) if n & (n - 1) == 0 else lax.rem(x, jnp.int32(n))


def _to_bf16(src_refs, dst_refs):
    out = []
    for src, dst in zip(src_refs, dst_refs):
        w = src[...].astype(BF16)
        dst[...] = w
        out.append(w)
    return out


def _whole(shape, n_grid_axes):
    return pl.BlockSpec(shape, lambda *_: (0,) * len(shape))


def _bias_tables_kernel(rb_ref, rbl_ref, near_ref, new_ref, tail_ref, far_ref, *, n_heads, ls):
    head = pl.program_id(0)

    def bucket_of(n):
        n = jnp.maximum(n, 0)
        nf = jnp.maximum(n, 1).astype(F32)
        large = MAX_EXACT + (
            jnp.log(nf / MAX_EXACT) / math.log(MAX_DISTANCE / MAX_EXACT) * (N_BUCKETS - MAX_EXACT)
        ).astype(jnp.int32)
        large = jnp.minimum(large, N_BUCKETS - 1)
        return jnp.where(n < MAX_EXACT, n, large)

    side = MAX_DISTANCE
    delta = lax.broadcasted_iota(jnp.int32, (side, side), 0) - lax.broadcasted_iota(jnp.int32, (side, side), 1)
    patterns = {}
    for diagonal in (0, 1):
        bucket = bucket_of(delta + diagonal * side)
        block = jnp.zeros((side, side), F32)
        for b in range(N_BUCKETS):
            block = jnp.where(bucket == b, rb_ref[b, head], block)
        patterns[diagonal] = block * LOG2E
    far = jnp.full((side, side), rb_ref[N_BUCKETS - 1, head] * LOG2E, F32)
    for bi in range(near_ref.shape[1] // side):
        for bj in range(near_ref.shape[2] // side):
            near_ref[0, bi * side:(bi + 1) * side, bj * side:(bj + 1) * side] = patterns.get(bi - bj + 1, far)

    @pl.when(head == 0)
    def _():
        n_q = n_heads * ls

        def slots(shape):
            row = lax.broadcasted_iota(jnp.int32, shape, 0)
            lane = lax.broadcasted_iota(jnp.int32, shape, 1)
            valid = jnp.logical_and(_rem(row, n_heads) == _div(_rem(lane, n_q), ls), lane < 2 * n_q)
            return _div(row, n_heads), _rem(lane, ls), valid

        def paged(ref, offset):
            key, query, valid = slots(ref.shape)
            bucket = bucket_of(query + offset - key)
            bias = jnp.zeros(ref.shape, F32)
            for b in range(N_BUCKETS):
                bias = jnp.where(bucket == b, rbl_ref[b:b + 1, :], bias)
            ref[...] = jnp.where(valid, bias * LOG2E, NEG_INF)

        paged(new_ref, 0)
        paged(tail_ref, PAGE_SIZE)
        tile = (V7X_SUBLANES, V7X_LANES)
        _, _, valid = slots(tile)
        far = jnp.broadcast_to(rbl_ref[N_BUCKETS - 1:N_BUCKETS, :] * LOG2E, tile)
        far_ref[0:8] = jnp.where(valid, far, NEG_INF)
        far_ref[8:16] = jnp.where(valid, far, 0.0)
        far_ref[16:24] = jnp.where(valid, NEG_INF, 0.0)
        far_ref[24:32] = jnp.where(valid, 1.0, 0.0)


def _bias_tables(rel_bias, tq, ls):
    n_heads = rel_bias.shape[1]
    n_q = n_heads * ls
    assert n_heads == V7X_SUBLANES and 2 * n_q <= V7X_LANES
    near_w = tq + MAX_DISTANCE
    page_rows = PAGE_SIZE * n_heads
    new_rows = -(-n_q // V7X_LANES) * V7X_LANES
    lane_head = (jnp.arange(V7X_LANES) % n_q) // ls
    rb_lanes = jnp.where(jnp.arange(V7X_LANES) < 2 * n_q, rel_bias[:, lane_head], 0.0)
    return pl.pallas_call(
        functools.partial(_bias_tables_kernel, n_heads=n_heads, ls=ls),
        grid=(n_heads,),
        in_specs=[pl.BlockSpec(memory_space=pltpu.SMEM), _whole((N_BUCKETS, V7X_LANES), 1)],
        out_specs=[
            pl.BlockSpec((1, tq, near_w), lambda h: (h, 0, 0)),
            _whole((new_rows, V7X_LANES), 1), _whole((page_rows, V7X_LANES), 1),
            _whole((4 * V7X_SUBLANES, V7X_LANES), 1),
        ],
        out_shape=[
            jax.ShapeDtypeStruct((n_heads, tq, near_w), F32),
            jax.ShapeDtypeStruct((new_rows, V7X_LANES), F32),
            jax.ShapeDtypeStruct((page_rows, V7X_LANES), F32),
            jax.ShapeDtypeStruct((4 * V7X_SUBLANES, V7X_LANES), F32),
        ],
        compiler_params=_params("arbitrary"),
        name="bias_tables",
    )(rel_bias, rb_lanes)


def _gated_conv_chunk(xn, wu, wb, wc, wz, cw, t, halo0, halo1):
    u = _dot(xn, wu)
    gate_b = _dot(xn, wb)
    gate_c = _dot(xn, wc)
    z = _dot(xn, wz)
    cu = gate_c * u
    prev1 = jnp.where(t >= 1, pltpu.roll(cu, 1, 0), halo1)
    prev2 = jnp.where(t >= 2, pltpu.roll(cu, 2, 0), jnp.where(t == 1, halo1, halo0))
    conv = prev2 * cw[0:1] + prev1 * cw[1:2] + cu * cw[2:3]
    return cu, gate_b * conv * _silu(z)


def _conv_prompt_kernel(*refs, tiles_per_seq, from_input):
    if from_input:
        x_ref, meta_ref, *refs = refs
    else:
        h_ref, *refs = refs
    nw_ref, wu_ref, wb_ref, wc_ref, wz_ref, cw_ref, wo_ref, o_ref, tail_ref, xn_scr, carry_scr = refs
    i, j = pl.program_id(0), pl.program_id(1)
    tm, tn = o_ref.shape[0], wu_ref.shape[1]
    starts_seq = i % tiles_per_seq == 0

    def start_tile(h):
        xn_scr[...] = _rms_norm(h, nw_ref[...]).astype(BF16)
        o_ref[...] = h

    if from_input:
        @pl.when(jnp.logical_and(j == 0, starts_seq))
        def _():
            start_tile(jnp.concatenate([meta_ref[...], x_ref[0:tm - N_META, :]], axis=0))

        @pl.when(jnp.logical_and(j == 0, jnp.logical_not(starts_seq)))
        def _():
            start_tile(x_ref[...])
    else:
        @pl.when(j == 0)
        def _():
            start_tile(h_ref[...])

    @pl.when(jnp.logical_and(j == 0, starts_seq))
    def _():
        carry_scr[...] = jnp.zeros(carry_scr.shape, F32)

    t = lax.broadcasted_iota(jnp.int32, (tm, tn), 0)
    halo0 = carry_scr[j, V7X_SUBLANES - 2:V7X_SUBLANES - 1, :]
    halo1 = carry_scr[j, V7X_SUBLANES - 1:V7X_SUBLANES, :]
    cu, y = _gated_conv_chunk(xn_scr[...], wu_ref[...], wb_ref[...], wc_ref[...], wz_ref[...], cw_ref[...],
                              t, halo0, halo1)
    last_rows = cu[tm - V7X_SUBLANES:tm]
    carry_scr[j] = last_rows
    tail_ref[0] = last_rows
    o_ref[...] += _dot(y.astype(BF16), wo_ref[...])


def _conv_sample_kernel(h_ref, nw_ref, wu_ref, wb_ref, wc_ref, wz_ref, cw_ref, wo_ref, halo0_ref, halo1_ref,
                        o_ref, cu_ref, wu_out, wb_out, wc_out, wz_out, wo_out, xn_scr, *, seg_rows):
    j = pl.program_id(0)
    tm, tn = h_ref.shape[0], wu_ref.shape[1]

    @pl.when(j == 0)
    def _():
        h = h_ref[...]
        xn_scr[...] = _rms_norm(h, nw_ref[...]).astype(BF16)
        o_ref[...] = h

    wu, wb, wc, wz, wo = _to_bf16((wu_ref, wb_ref, wc_ref, wz_ref, wo_ref), (wu_out, wb_out, wc_out, wz_out, wo_out))
    t = _rem(lax.broadcasted_iota(jnp.int32, (tm, tn), 0), seg_rows)
    cu, y = _gated_conv_chunk(xn_scr[...], wu, wb, wc, wz, cw_ref[...], t, halo0_ref[...], halo1_ref[...])
    cu_ref[...] = cu
    o_ref[...] += _dot(y.astype(BF16), wo)


def _w_in_specs(layer, d, tn, col_map):
    n_chunks = d // tn
    return [
        pl.BlockSpec((None, d, tn), functools.partial(col_map, layer, c * n_chunks)) for c in range(4)
    ]


def _conv_layer_prompt(h, norm_w, w_in_groups, conv_w, w_out, layer, batch, seq_len):
    from_input = isinstance(h, tuple)
    d = w_out.shape[0]
    n_rows = batch * seq_len
    tm, tn = _row_tile(seq_len, ROW_TILE_CAP), PROMPT_COL_TILE
    tiles_per_seq = seq_len // tm
    n_tiles, n_chunks = n_rows // tm, d // tn
    if from_input:
        x_len = seq_len - N_META
        assert tm > N_META and tiles_per_seq * tm - N_META == x_len

        assert x_len % V7X_BF16_SUBLANES == 0 and N_META % V7X_BF16_SUBLANES == 0

        def x_window(i, j):
            row = (i // tiles_per_seq) * x_len + jnp.maximum((i % tiles_per_seq) * tm - N_META, 0)
            return pl.multiple_of(row, V7X_BF16_SUBLANES), 0

        h_specs = [pl.BlockSpec((pl.Element(tm), pl.Element(d)), x_window), _whole((N_META, d), 2)]
        h_args = list(h)
    else:
        h_specs = [pl.BlockSpec((tm, d), lambda i, j: (i, 0))]
        h_args = [h]
    w_chunk = pl.BlockSpec((d, tn), lambda i, j: (0, j))
    out, tails = pl.pallas_call(
        functools.partial(_conv_prompt_kernel, tiles_per_seq=tiles_per_seq, from_input=from_input),
        grid=(n_tiles, n_chunks),
        in_specs=[
            *h_specs,
            _whole((1, d), 2),
            w_chunk, w_chunk, w_chunk, w_chunk,
            pl.BlockSpec((None, CONV_WIDTH, tn), lambda i, j: (layer, 0, j)),
            pl.BlockSpec((tn, d), lambda i, j: (j, 0)),
        ],
        out_specs=[
            pl.BlockSpec((tm, d), lambda i, j: (i, 0)),
            pl.BlockSpec((1, V7X_SUBLANES, tn), lambda i, j: (i, 0, j)),
        ],
        out_shape=[
            jax.ShapeDtypeStruct((n_rows, d), F32),
            jax.ShapeDtypeStruct((n_tiles, V7X_SUBLANES, d), F32),
        ],
        scratch_shapes=[
            pltpu.VMEM((tm, d), BF16),
            pltpu.VMEM((n_chunks, V7X_SUBLANES, tn), F32),
        ],
        compiler_params=_params("arbitrary", "arbitrary"),
        name="conv_mixer_prompt",
    )(*h_args, norm_w, *w_in_groups, conv_w, w_out)
    state = tails.reshape(batch, tiles_per_seq, V7X_SUBLANES, d)[:, -1, V7X_SUBLANES - (CONV_WIDTH - 1):]
    return out, state


def _conv_layer_sample(h, norm_w, w_in, conv_w, w_out, layer, state, seq_len):
    n_rows, d = h.shape
    tn = COL_TILE
    n_chunks = d // tn
    n_seq = n_rows // seq_len
    halo0 = jnp.repeat(state[:, 0], seq_len, axis=0)
    halo1 = jnp.repeat(state[:, 1], seq_len, axis=0)
    rows_chunk = pl.BlockSpec((n_rows, tn), lambda j: (0, j))
    w_chunk = pl.BlockSpec((d, tn), lambda j: (0, j))
    w_bf16 = jax.ShapeDtypeStruct((d, d), BF16)
    out, cu, *weights = pl.pallas_call(
        functools.partial(_conv_sample_kernel, seg_rows=seq_len),
        grid=(n_chunks,),
        in_specs=[
            _whole((n_rows, d), 1),
            _whole((1, d), 1),
            *_w_in_specs(layer, d, tn, lambda l, c0, j: (l, 0, c0 + j)),
            pl.BlockSpec((None, CONV_WIDTH, tn), lambda j: (layer, 0, j)),
            pl.BlockSpec((None, tn, d), lambda j: (layer, j, 0)),
            rows_chunk, rows_chunk,
        ],
        out_specs=[
            _whole((n_rows, d), 1), rows_chunk,
            w_chunk, w_chunk, w_chunk, w_chunk,
            pl.BlockSpec((tn, d), lambda j: (j, 0)),
        ],
        out_shape=[
            jax.ShapeDtypeStruct((n_rows, d), F32),
            jax.ShapeDtypeStruct((n_rows, d), F32),
            w_bf16, w_bf16, w_bf16, w_bf16, w_bf16,
        ],
        scratch_shapes=[pltpu.VMEM((n_rows, d), BF16)],
        compiler_params=_params("arbitrary"),
        name="conv_mixer_sample",
    )(h, norm_w, w_in, w_in, w_in, w_in, conv_w, w_out, halo0, halo1)
    new_state = cu.reshape(n_seq, seq_len, d)[:, seq_len - (CONV_WIDTH - 1):]
    return out, new_state, weights[:4], weights[4]


def _attn_inproj_sample_kernel(h_ref, nw_ref, wq_ref, wk_ref, wv_ref, wz_ref,
                               q_ref, k_ref, v_ref, z_ref, wq_out, wk_out, wv_out, wz_out, xn_scr):
    @pl.when(pl.program_id(0) == 0)
    def _():
        xn_scr[...] = _rms_norm(h_ref[...], nw_ref[...]).astype(BF16)

    wq, wk, wv, wz = _to_bf16((wq_ref, wk_ref, wv_ref, wz_ref), (wq_out, wk_out, wv_out, wz_out))
    xn = xn_scr[...]
    q_ref[...] = _dot(xn, wq) * SCORE_SCALE
    k_ref[...] = _dot(xn, wk)
    v_ref[...] = _dot(xn, wv)
    z_ref[...] = _dot(xn, wz)


def _attn_inproj_sample(h, norm_w, w_in, layer):
    n_rows, d = h.shape
    tn = COL_TILE
    chunk = pl.BlockSpec((n_rows, tn), lambda j: (0, j))
    w_chunk = pl.BlockSpec((d, tn), lambda j: (0, j))
    outs = pl.pallas_call(
        _attn_inproj_sample_kernel,
        grid=(d // tn,),
        in_specs=[_whole((n_rows, d), 1), _whole((1, d), 1),
                  *_w_in_specs(layer, d, tn, lambda l, c0, j: (l, 0, c0 + j))],
        out_specs=[chunk] * 4 + [w_chunk] * 4,
        out_shape=[jax.ShapeDtypeStruct((n_rows, d), F32)] * 4 + [jax.ShapeDtypeStruct((d, d), BF16)] * 4,
        scratch_shapes=[pltpu.VMEM((n_rows, d), BF16)],
        compiler_params=_params("arbitrary"),
        name="attn_inproj_sample",
    )(h, norm_w, w_in, w_in, w_in, w_in)
    return outs[:4], outs[4:]


def _attn_inproj_prompt_kernel(h_ref, nw_ref, wq_ref, wk_ref, wv_ref, wz_ref, *rest, n_earlier):
    earlier, rest = rest[:2 * n_earlier], rest[2 * n_earlier:]
    q_ref, kb_ref, vb_ref, z_ref, k_out, v_out, xn_scr, chunk_buf, chunk_sem, *staging = rest
    i, j = pl.program_id(0), pl.program_id(1)
    n_tiles, n_chunks = pl.num_programs(0), pl.num_programs(1)
    tm = chunk_buf.shape[2]
    step = i * n_chunks + j
    last_step = step == n_tiles * n_chunks - 1
    slot = lax.rem(step, 2)

    def chunk_copies(at_step, buf_slot):
        rows = pl.ds(pl.multiple_of((at_step // n_chunks) * tm, V7X_SUBLANES), tm)
        heads_per_chunk = chunk_buf.shape[3] // HEAD_WIDTH
        for c, out in enumerate((k_out, v_out)):
            for hh in range(heads_per_chunk):
                head = lax.rem(at_step, n_chunks) * heads_per_chunk + hh
                src = chunk_buf.at[c, buf_slot, :, hh * HEAD_WIDTH:(hh + 1) * HEAD_WIDTH]
                dst = out.at[n_earlier, rows, head, :] if n_earlier else out.at[rows, head, :]
                yield pltpu.make_async_copy(src, dst, chunk_sem.at[c, buf_slot, hh])

    def slab_copies(slab, inward):
        stage, in_sem, out_sem = staging
        slab_rows = stage.shape[2]
        rows = pl.ds(slab * slab_rows, slab_rows)
        slab_slot = lax.rem(slab, 2)
        for a, src in enumerate(earlier):
            buf = stage.at[a, slab_slot]
            if inward:
                yield pltpu.make_async_copy(src.at[rows], buf, in_sem.at[a, slab_slot])
            else:
                yield pltpu.make_async_copy(buf, (k_out, v_out)[a % 2].at[a // 2, rows], out_sem.at[a, slab_slot])

    def start(copies):
        for copy in copies:
            copy.start()

    def wait(copies):
        for copy in copies:
            copy.wait()

    if n_earlier:
        @pl.when(step == 0)
        def _():
            start(slab_copies(0, inward=True))

        @pl.when(step == 1)
        def _():
            wait(slab_copies(0, inward=True))
            start(slab_copies(0, inward=False))
            start(slab_copies(1, inward=True))

    @pl.when(step >= 2)
    def _():
        wait(chunk_copies(step - 2, slot))
        if n_earlier:
            wait(slab_copies(step - 2, inward=False))
            wait(slab_copies(step - 1, inward=True))
            start(slab_copies(step - 1, inward=False))
            start(slab_copies(step, inward=True))

    @pl.when(j == 0)
    def _():
        xn_scr[...] = _rms_norm(h_ref[...], nw_ref[...]).astype(BF16)

    xn = xn_scr[...]
    q_ref[...] = (_dot(xn, wq_ref[...]) * SCORE_SCALE).astype(BF16)
    z_ref[...] = _dot(xn, wz_ref[...]).astype(BF16)
    k = _dot(xn, wk_ref[...])
    v = _dot(xn, wv_ref[...])
    kb_ref[...] = k.astype(BF16)
    vb_ref[...] = v.astype(BF16)
    chunk_buf[0, slot] = k
    chunk_buf[1, slot] = v
    start(chunk_copies(step, slot))

    @pl.when(last_step)
    def _():
        wait(chunk_copies(step, slot))
        wait(chunk_copies(step - 1, 1 - slot))
        if n_earlier:
            wait(slab_copies(step, inward=True))
            start(slab_copies(step, inward=False))
            wait(slab_copies(step - 1, inward=False))
            wait(slab_copies(step, inward=False))


def _attn_inproj_prompt(h, norm_w, weights, tm, earlier_kv=()):
    n_rows, d = h.shape
    tn = PROMPT_COL_TILE
    assert tn % HEAD_WIDTH == 0
    n_heads = d // HEAD_WIDTH
    n_earlier = len(earlier_kv)
    n_steps = (n_rows // tm) * (d // tn)
    assert n_steps >= 2 and n_rows % n_steps == 0
    chunk = pl.BlockSpec((tm, tn), lambda i, j: (i, j))
    w_chunk = pl.BlockSpec((d, tn), lambda i, j: (0, j))
    in_hbm = pl.BlockSpec(memory_space=pl.ANY)
    act = jax.ShapeDtypeStruct((n_rows, d), BF16)
    kv_shape = (n_rows, n_heads, HEAD_WIDTH)
    scratch = [pltpu.VMEM((tm, d), BF16), pltpu.VMEM((2, 2, tm, tn), F32),
               pltpu.SemaphoreType.DMA((2, 2, tn // HEAD_WIDTH))]
    if n_earlier:
        kv_shape = (n_earlier + 1, *kv_shape)
        scratch += [
            pltpu.VMEM((2 * n_earlier, 2, n_rows // n_steps, n_heads, HEAD_WIDTH), F32),
            pltpu.SemaphoreType.DMA((2 * n_earlier, 2)),
            pltpu.SemaphoreType.DMA((2 * n_earlier, 2)),
        ]
    kv = jax.ShapeDtypeStruct(kv_shape, F32)
    return pl.pallas_call(
        functools.partial(_attn_inproj_prompt_kernel, n_earlier=n_earlier),
        grid=(n_rows // tm, d // tn),
        in_specs=[pl.BlockSpec((tm, d), lambda i, j: (i, 0)), _whole((1, d), 2), w_chunk, w_chunk, w_chunk, w_chunk]
        + [in_hbm] * (2 * n_earlier),
        out_specs=[chunk, chunk, chunk, chunk, in_hbm, in_hbm],
        out_shape=[act, act, act, act, kv, kv],
        scratch_shapes=scratch,
        compiler_params=_params("arbitrary", "arbitrary"),
        name="attn_inproj_prompt",
    )(h, norm_w, *weights, *[a for pair in earlier_kv for a in pair])


def _prompt_attn_kernel(rb_ref, q_ref, k_ref, v_ref, z_ref, near_ref, lam_ref, sw_ref,
                        y_ref, kb_scr, vb_scr, s_scr, a_scr, *, lam_init):
    head = pl.program_id(1)
    seq_len = q_ref.shape[1]
    padded_len = kb_scr.shape[0]
    tq = near_ref.shape[1]
    kb_scr[0:seq_len] = k_ref[0]
    vb_scr[0:seq_len] = v_ref[0]
    if padded_len > seq_len:
        kb_scr[seq_len:padded_len] = jnp.zeros((padded_len - seq_len, HEAD_WIDTH), BF16)
        vb_scr[seq_len:padded_len] = jnp.zeros((padded_len - seq_len, HEAD_WIDTH), BF16)
    lam = _lambda(lam_ref[...], lam_init)
    far_bias = rb_ref[N_BUCKETS - 1, head] * LOG2E

    for r0 in range(0, seq_len, tq):
        rows = min(tq, seq_len - r0)
        near_lo = max(r0 - MAX_DISTANCE, 0)
        near_hi = r0 + tq
        near_w = near_hi - near_lo
        col0 = near_lo - (r0 - MAX_DISTANCE)
        q = q_ref[0, r0:r0 + rows, :]
        row = lax.broadcasted_iota(jnp.int32, (rows, near_w), 0)
        col = lax.broadcasted_iota(jnp.int32, (rows, near_w), 1)
        visible = col + col0 <= row + MAX_DISTANCE
        near_bias = near_ref[0, 0:rows, col0:col0 + near_w]
        shifts, denom = [], []
        for c in range(2):
            lanes = slice(c * HEAD_DIM, (c + 1) * HEAD_DIM)
            s_near = _dot_nt(q[:, lanes], kb_scr[near_lo:near_hi, lanes])
            s_near = jnp.where(visible, s_near + near_bias, NEG_INF)
            s_scr[c, 0:rows, near_lo:near_hi] = s_near
            m = jnp.max(s_near, axis=-1, keepdims=True)
            if near_lo > 0:
                s_far = _dot_nt(q[:, lanes], kb_scr[0:near_lo, lanes])
                s_scr[c, 0:rows, 0:near_lo] = s_far
                m = jnp.maximum(m, jnp.max(s_far, axis=-1, keepdims=True) + far_bias)
            shifts.append(m)
        for c in range(2):
            m = shifts[c]
            e = jnp.exp2(s_scr[c, 0:rows, near_lo:near_hi] - m)
            total = jnp.sum(e, axis=-1, keepdims=True)
            s_scr[c, 0:rows, near_lo:near_hi] = e
            if near_lo > 0:
                e = jnp.exp2(s_scr[c, 0:rows, 0:near_lo] - (m - far_bias))
                total = total + jnp.sum(e, axis=-1, keepdims=True)
                s_scr[c, 0:rows, 0:near_lo] = e
            denom.append(total)
        w1 = 1.0 / denom[0]
        w2 = lam / denom[1]
        a_scr[0:rows, 0:near_hi] = (s_scr[0, 0:rows, 0:near_hi] * w1 - s_scr[1, 0:rows, 0:near_hi] * w2).astype(BF16)
        o = _dot(a_scr[0:rows, 0:near_hi], vb_scr[0:near_hi, :])
        z = z_ref[0, r0:r0 + rows, :].astype(F32)
        y_ref[0, r0:r0 + rows, :] = _sub_norm_gate(o, sw_ref[...], lam_init, z).astype(y_ref.dtype)


def _prompt_attention(q, k, v, z, rel_bias, near_table, lam_params, sub_w, lam_init):
    batch, seq_len, d = q.shape
    n_heads = d // HEAD_WIDTH
    tq = near_table.shape[1]
    padded_len = -(-seq_len // tq) * tq
    head_cols = pl.BlockSpec((1, seq_len, HEAD_WIDTH), lambda b, h: (b, 0, h))
    return pl.pallas_call(
        functools.partial(_prompt_attn_kernel, lam_init=lam_init),
        grid=(batch, n_heads),
        in_specs=[
            pl.BlockSpec(memory_space=pltpu.SMEM),
            head_cols, head_cols, head_cols, head_cols,
            pl.BlockSpec((1, tq, tq + MAX_DISTANCE), lambda b, h: (h, 0, 0)),
            _whole((4, HEAD_DIM), 2),
            _whole((1, HEAD_WIDTH), 2),
        ],
        out_specs=head_cols,
        out_shape=jax.ShapeDtypeStruct((batch, seq_len, d), BF16),
        scratch_shapes=[
            pltpu.VMEM((padded_len, HEAD_WIDTH), BF16),
            pltpu.VMEM((padded_len, HEAD_WIDTH), BF16),
            pltpu.VMEM((2, tq, padded_len), F32),
            pltpu.VMEM((tq, padded_len), BF16),
        ],
        compiler_params=_params("arbitrary", "arbitrary"),
        name="prompt_attention",
    )(rel_bias, q, k, v, z, near_table, lam_params, sub_w)


def _decode_attn_kernel(pt_ref, q_ref, kn_ref, vn_ref, z_ref, *rest, n_pg, lam_init):
    k_pages, v_pages = rest[:n_pg], rest[n_pg:2 * n_pg]
    (new_ref, tail_ref, far_ref, lam_ref, sw_ref, y_ref,
     s_scr, m_seen_scr, m_used_scr, l_scr, acc_scr) = rest[2 * n_pg:]
    del pt_ref
    step, n_groups = pl.program_id(1), pl.num_programs(1) - 1
    ls, d = z_ref.shape
    n_heads = d // HEAD_WIDTH
    n_q = n_heads * ls
    page_rows = PAGE_SIZE * n_heads
    tile = (V7X_SUBLANES, V7X_LANES)
    q_cols = q_ref[0]
    far_bias, far_bias_or_0, m_start, valid = far_ref[0:8], far_ref[8:16], far_ref[16:24], far_ref[24:32]
    slot = lax.rem(step, 2)

    def per_tile(x):
        return x.reshape(x.shape[0] // V7X_SUBLANES, *tile)

    def lanes_to_rows(stat):
        per_lane = jnp.sum(stat * valid, axis=0, keepdims=True)
        return jnp.broadcast_to(per_lane, tile).T[:, 0:1]

    def fold(probs, values, m_from, m_to):
        alpha = jnp.exp2(m_from - m_to)
        total = alpha * l_scr[...]
        pv = None
        for p, vb in zip(probs, values):
            total = total + jnp.sum(p, axis=0)
            part = lax.dot_general(p.reshape(vb.shape[0], V7X_LANES), vb, (((0,), (0,)), ((), ())),
                                   preferred_element_type=F32)
            pv = part if pv is None else pv + part
        l_scr[...] = total
        acc_scr[...] = lanes_to_rows(alpha) * acc_scr[...] + pv

    @pl.when(step == 0)
    def _():
        m_seen_scr[...] = m_start
        m_used_scr[...] = m_start
        l_scr[...] = jnp.zeros(l_scr.shape, F32)
        acc_scr[...] = jnp.zeros(acc_scr.shape, F32)
        s_scr[1] = jnp.full(s_scr.shape[1:], NEG_INF, F32)

    keys_are_tail = step >= n_groups - 1
    values_are_tail = step == n_groups

    m_use = m_seen_scr[...]
    prev = s_scr.at[1 - slot]
    probs = []
    for r in range(n_pg):
        s = per_tile(prev[r * page_rows:(r + 1) * page_rows])
        if r == n_pg - 1:
            probs.append(jnp.exp2(s + jnp.where(values_are_tail, per_tile(tail_ref[...]), far_bias) - m_use))
        else:
            probs.append(jnp.exp2(s + (far_bias - m_use)))
    fold(probs, [ref[0, 0].reshape(page_rows, HEAD_WIDTH) for ref in v_pages], m_used_scr[...], m_use)
    m_used_scr[...] = m_use

    m_new = m_use
    for r, ref in enumerate(k_pages):
        logits = _dot(ref[0, 0].reshape(page_rows, HEAD_WIDTH), q_cols)
        s_scr[slot, r * page_rows:(r + 1) * page_rows] = logits
        s = per_tile(logits)
        if r == n_pg - 1:
            s = s + jnp.where(keys_are_tail, per_tile(tail_ref[...]), far_bias_or_0)
            m_new = jnp.maximum(m_new, jnp.max(s, axis=0))
        else:
            m_new = jnp.maximum(m_new, jnp.max(s, axis=0) + far_bias_or_0)
    m_seen_scr[...] = m_new

    @pl.when(step == n_groups)
    def _():
        shape = new_ref.shape
        key = _div(lax.broadcasted_iota(jnp.int32, shape, 0), n_heads)
        query = _rem(lax.broadcasted_iota(jnp.int32, shape, 1), ls)
        s = per_tile(_dot(kn_ref[0], q_cols) + jnp.where(key <= query, new_ref[...], NEG_INF))
        m_all = jnp.maximum(m_use, jnp.max(s, axis=0))
        fold([jnp.exp2(s - m_all)], [vn_ref[0]], m_use, m_all)
        lam = _lambda(lam_ref[...], lam_init)
        o = acc_scr[...] / lanes_to_rows(l_scr[...])
        o = o[0:n_q] - lam * o[n_q:2 * n_q]
        o = o * lax.rsqrt(jnp.mean(o * o, axis=-1, keepdims=True) + EPS)
        o = o * sw_ref[...] * (1.0 - lam_init)
        for h in range(n_heads):
            cols = slice(h * HEAD_WIDTH, (h + 1) * HEAD_WIDTH)
            y_ref[:, cols] = o[h * ls:(h + 1) * ls] * _silu(z_ref[:, cols])


def _decode_attention(q, k_new, v_new, z, cache_k, cache_v, page_table, layer, new_table, tail_table, far_table,
                      lam_params, sub_w, lam_init, ls):
    n_rows, d = q.shape
    batch, n_pages = page_table.shape
    n_heads = d // HEAD_WIDTH
    n_q = n_heads * ls
    new_rows = new_table.shape[0]
    n_pg = math.gcd(n_pages, PAGES_PER_STEP)
    n_groups = n_pages // n_pg
    q_t = q.reshape(batch, ls, n_heads, 2, HEAD_DIM).transpose(0, 3, 4, 2, 1).reshape(batch, 2, HEAD_DIM, n_q)
    q_cols = jnp.zeros((batch, HEAD_WIDTH, V7X_LANES), F32)
    q_cols = q_cols.at[:, :HEAD_DIM, :n_q].set(q_t[:, 0]).at[:, HEAD_DIM:, n_q:2 * n_q].set(q_t[:, 1])
    pad = ((0, 0), (0, new_rows - n_q), (0, 0))
    k_rows = jnp.pad(k_new.reshape(batch, n_q, HEAD_WIDTH), pad)
    v_rows = jnp.pad(v_new.reshape(batch, n_q, HEAD_WIDTH), pad)

    def page_specs(group_of_step):
        return [
            pl.BlockSpec((1, 1, PAGE_SIZE, n_heads, HEAD_WIDTH), functools.partial(
                lambda r, b, s, pt: (layer, pt[b, group_of_step(s) * n_pg + r], 0, 0, 0), r))
            for r in range(n_pg)
        ]

    new_page = pl.BlockSpec((1, new_rows, HEAD_WIDTH), lambda b, s, pt: (b, 0, 0))
    grid_spec = pltpu.PrefetchScalarGridSpec(
        num_scalar_prefetch=1,
        grid=(batch, n_groups + 1),
        in_specs=[
            pl.BlockSpec((1, HEAD_WIDTH, V7X_LANES), lambda b, s, pt: (b, 0, 0)),
            new_page, new_page,
            pl.BlockSpec((ls, d), lambda b, s, pt: (b, 0)),
            *page_specs(lambda s: jnp.minimum(s, n_groups - 1)), *page_specs(lambda s: jnp.maximum(s - 1, 0)),
            _whole(new_table.shape, 3), _whole(tail_table.shape, 3), _whole(far_table.shape, 3),
            _whole((4, HEAD_DIM), 3), _whole((1, HEAD_WIDTH), 3),
        ],
        out_specs=pl.BlockSpec((ls, d), lambda b, s, pt: (b, 0)),
        scratch_shapes=[
            pltpu.VMEM((2, n_pg * PAGE_SIZE * n_heads, V7X_LANES), F32),
            pltpu.VMEM((V7X_SUBLANES, V7X_LANES), F32),
            pltpu.VMEM((V7X_SUBLANES, V7X_LANES), F32),
            pltpu.VMEM((V7X_SUBLANES, V7X_LANES), F32),
            pltpu.VMEM((V7X_LANES, HEAD_WIDTH), F32),
        ],
    )
    return pl.pallas_call(
        functools.partial(_decode_attn_kernel, n_pg=n_pg, lam_init=lam_init),
        grid_spec=grid_spec,
        out_shape=jax.ShapeDtypeStruct((n_rows, d), F32),
        compiler_params=_params("arbitrary", "arbitrary"),
        name="decode_attention",
    )(page_table, q_cols, k_rows, v_rows, z, *([cache_k] * n_pg), *([cache_v] * n_pg),
      new_table, tail_table, far_table, lam_params, sub_w)


def _outproj_kernel(y_ref, w_ref, h_ref, *rest, final_norm):
    h = h_ref[...] + _dot(y_ref[...], w_ref[...])
    if final_norm:
        fw_ref, o_ref = rest
        o_ref[...] = _rms_norm(h, fw_ref[...])
    else:
        (o_ref,) = rest
        o_ref[...] = h


def _outproj_prompt(y, w_out, h, batch, seq_len, final_w=None):
    n_rows, d = h.shape
    w_spec = _whole((d, d), 1)
    if final_w is None:
        tm = _row_tile(seq_len, ROW_TILE_CAP)
        rows = pl.BlockSpec((tm, d), lambda i: (i, 0))
        return pl.pallas_call(
            functools.partial(_outproj_kernel, final_norm=False),
            grid=(n_rows // tm,),
            in_specs=[rows, w_spec, rows],
            out_specs=rows,
            out_shape=jax.ShapeDtypeStruct((n_rows, d), F32),
            compiler_params=_params("arbitrary"),
            name="attn_outproj",
        )(y, w_out, h)
    out_len = seq_len - N_META
    tm = _row_tile(out_len, ROW_TILE_CAP)
    tiles_per_seq = out_len // tm
    assert seq_len % V7X_BF16_SUBLANES == 0 and N_META % V7X_BF16_SUBLANES == 0

    def rows_window(i):
        row = (i // tiles_per_seq) * seq_len + N_META + (i % tiles_per_seq) * tm
        return pl.multiple_of(row, V7X_BF16_SUBLANES), 0

    window = pl.BlockSpec((pl.Element(tm), pl.Element(d)), rows_window)
    return pl.pallas_call(
        functools.partial(_outproj_kernel, final_norm=True),
        grid=(batch * tiles_per_seq,),
        in_specs=[window, w_spec, window, _whole((1, d), 1)],
        out_specs=pl.BlockSpec((tm, d), lambda i: (i, 0)),
        out_shape=jax.ShapeDtypeStruct((batch * out_len, d), F32),
        compiler_params=_params("arbitrary"),
        name="attn_outproj_final",
    )(y, w_out, h, final_w)


def _outproj_sample_kernel(y_ref, w_ref, h_ref, o_ref, w_out):
    (w,) = _to_bf16((w_ref,), (w_out,))
    o_ref[...] = h_ref[...] + _dot(y_ref[...].astype(BF16), w)


def _outproj_sample(y, w_out, layer, h):
    n_rows, d = h.shape
    tn = OUT_COL_TILE
    cols = pl.BlockSpec((n_rows, tn), lambda j: (0, j))
    return pl.pallas_call(
        _outproj_sample_kernel,
        grid=(d // tn,),
        in_specs=[_whole((n_rows, d), 1), pl.BlockSpec((None, d, tn), lambda j: (layer, 0, j)), cols],
        out_specs=[cols, pl.BlockSpec((d, tn), lambda j: (0, j))],
        out_shape=[jax.ShapeDtypeStruct((n_rows, d), F32), jax.ShapeDtypeStruct((d, d), BF16)],
        compiler_params=_params("arbitrary"),
        name="attn_outproj_sample",
    )(y, w_out, h)


def _final_norm_kernel(h_ref, w_ref, o_ref):
    o_ref[...] = _rms_norm(h_ref[...], w_ref[...])


def _final_norm(h, w):
    return pl.pallas_call(
        _final_norm_kernel, out_shape=jax.ShapeDtypeStruct(h.shape, F32), name="final_norm_sample")(h, w)


def kernel(x_prompt, x_sample, state_conv, cache_k, cache_v, page_table, meta_tokens, rel_bias, norm_w,
           final_norm_w, conv_w_in, conv_w, conv_w_out, attn_w_in, attn_lambda, attn_subln_w, attn_w_out):
    b, seq, d = x_prompt.shape
    db, ls, _ = x_sample.shape
    depth = norm_w.shape[0]
    lp = N_META + seq
    n_heads = d // HEAD_WIDTH
    assert d % HEAD_WIDTH == 0 and d % COL_TILE == 0 and d % OUT_COL_TILE == 0
    assert ls >= CONV_WIDTH - 1 and (db * ls) % V7X_BF16_SUBLANES == 0
    assert cache_k.shape[2:] == (PAGE_SIZE, n_heads, HEAD_WIDTH) and cache_v.shape == cache_k.shape
    assert ls <= PAGE_SIZE
    assert depth % 2 == 0, "the final norm is fused into the last attention mixer's out-projection"

    hp = (x_prompt.reshape(b * seq, d), meta_tokens.astype(x_prompt.dtype))
    hs = x_sample.reshape(db * ls, d)
    tm_p = _row_tile(lp, ROW_TILE_CAP)
    fw = final_norm_w.reshape(1, d)

    near_table, new_table, tail_table, far_table = _bias_tables(rel_bias, Q_TILE, ls)

    conv_p, conv_s, k_s, v_s = [], [], [], []
    kv_p = []
    for i in range(depth):
        nw = norm_w[i].reshape(1, d)
        j = i // 2
        last = i == depth - 1
        if i % 2 == 0:
            hs, st_s, w_in_b, w_out_b = _conv_layer_sample(hs, nw, conv_w_in, conv_w, conv_w_out, j, state_conv[j], ls)
            hp, st_p = _conv_layer_prompt(hp, nw, w_in_b, conv_w, w_out_b, j, b, lp)
            conv_p.append(st_p)
            conv_s.append(st_s)
        else:
            lam_init = 0.8 - 0.6 * math.exp(-0.3 * i)
            sub_w = attn_subln_w[j].reshape(1, HEAD_WIDTH)
            (qs, ks, vs, zs), w_in_b = _attn_inproj_sample(hs, nw, attn_w_in, j)
            ys = _decode_attention(qs, ks, vs, zs, cache_k, cache_v, page_table, j, new_table, tail_table,
                                   far_table, attn_lambda[j], sub_w, lam_init, ls)
            hs, w_out_b = _outproj_sample(ys, attn_w_out, j, hs)
            last_attn = j == depth // 2 - 1
            q, k, v, z, k_out, v_out = _attn_inproj_prompt(hp, nw, w_in_b, tm_p, kv_p if last_attn else ())
            kv_p = [(k_out, v_out)] if last_attn else kv_p + [(k_out, v_out)]
            yp = _prompt_attention(q.reshape(b, lp, d), k.reshape(b, lp, d), v.reshape(b, lp, d),
                                   z.reshape(b, lp, d), rel_bias, near_table, attn_lambda[j], sub_w, lam_init)
            hp = _outproj_prompt(yp.reshape(b * lp, d), w_out_b, hp, b, lp, fw if last else None)
            k_s.append(ks.reshape(db, ls, n_heads, HEAD_WIDTH))
            v_s.append(vs.reshape(db, ls, n_heads, HEAD_WIDTH))

    y_prompt = hp.reshape(b, seq, d)
    y_sample = _final_norm(hs, fw).reshape(db, ls, d)
    k_p, v_p = (a.reshape(depth // 2, b, lp, n_heads, HEAD_WIDTH) for a in kv_p[0])
    return (y_prompt, y_sample, jnp.stack(conv_p), jnp.stack(conv_s), k_p, v_p, jnp.stack(k_s), jnp.stack(v_s))
```

```python
import functools
import math

import jax
import jax.numpy as jnp
from jax import lax
from jax.experimental import pallas as pl
from jax.experimental.pallas import tpu as pltpu

N_META = 16
HEAD_DIM = 128
HEAD_WIDTH = 2 * HEAD_DIM
PAGE_SIZE = 128
N_BUCKETS = 32
MAX_EXACT = N_BUCKETS // 2
MAX_DISTANCE = 128
CONV_WIDTH = 3
EPS = 1e-6
NEG_INF = -1e30
LOG2E = math.log2(math.e)
SCORE_SCALE = HEAD_DIM ** -0.5 * LOG2E

BF16 = jnp.bfloat16
F32 = jnp.float32

V7X_LANES = 128
V7X_SUBLANES = 8
V7X_BF16_SUBLANES = 16
V7X_VMEM_LIMIT_BYTES = 56 * 1024 * 1024

ROW_TILE_CAP = 768
COL_TILE = 256
PROMPT_COL_TILE = 512
OUT_COL_TILE = 512
Q_TILE = 512
PAGES_PER_STEP = 8


def _row_tile(n_rows, cap):
    best = None
    for t in range(V7X_BF16_SUBLANES, min(n_rows, cap) + 1, V7X_BF16_SUBLANES):
        if n_rows % t == 0:
            best = t
    return n_rows if best is None else best


def _params(*semantics):
    return pltpu.CompilerParams(dimension_semantics=semantics, vmem_limit_bytes=V7X_VMEM_LIMIT_BYTES)


def _dot(a, b):
    return jnp.dot(a, b, preferred_element_type=F32)


def _dot_nt(a, b):
    return lax.dot_general(a, b, (((1,), (1,)), ((), ())), preferred_element_type=F32)


def _rms_norm(x, w):
    return x * lax.rsqrt(jnp.mean(x * x, axis=-1, keepdims=True) + EPS) * w


def _silu(z):
    return z * (1.0 / (1.0 + jnp.exp(-z)))


def _lambda(lp, lam_init):
    a = jnp.sum(lp[0:1] * lp[1:2], axis=-1, keepdims=True)
    b = jnp.sum(lp[2:3] * lp[3:4], axis=-1, keepdims=True)
    return jnp.exp(a) - jnp.exp(b) + lam_init


def _sub_norm_gate(o, sub_w, lam_init, z):
    o = o * lax.rsqrt(jnp.mean(o * o, axis=-1, keepdims=True) + EPS)
    o = o * sub_w * (1.0 - lam_init)
    return o * _silu(z)


def _div(x, n):
    return x >> (n.bit_length() - 1) if n & (n - 1) == 0 else lax.div(x, jnp.int32(n))


def _rem(x, n):
    return x & (n - 1) if n & (n - 1) == 0 else lax.rem(x, jnp.int32(n))


def _to_bf16(src_refs, dst_refs):
    out = []
    for src, dst in zip(src_refs, dst_refs):
        w = src[...].astype(BF16)
        dst[...] = w
        out.append(w)
    return out


def _whole(shape, n_grid_axes):
    return pl.BlockSpec(shape, lambda *_: (0,) * len(shape))


def _bias_tables_kernel(rb_ref, rbl_ref, near_ref, new_ref, tail_ref, far_ref, *, n_heads, ls):
    head = pl.program_id(0)

    def bucket_of(n):
        n = jnp.maximum(n, 0)
        nf = jnp.maximum(n, 1).astype(F32)
        large = MAX_EXACT + (
            jnp.log(nf / MAX_EXACT) / math.log(MAX_DISTANCE / MAX_EXACT) * (N_BUCKETS - MAX_EXACT)
        ).astype(jnp.int32)
        large = jnp.minimum(large, N_BUCKETS - 1)
        return jnp.where(n < MAX_EXACT, n, large)

    side = MAX_DISTANCE
    delta = lax.broadcasted_iota(jnp.int32, (side, side), 0) - lax.broadcasted_iota(jnp.int32, (side, side), 1)
    patterns = {}
    for diagonal in (0, 1):
        bucket = bucket_of(delta + diagonal * side)
        block = jnp.zeros((side, side), F32)
        for b in range(N_BUCKETS):
            block = jnp.where(bucket == b, rb_ref[b, head], block)
        patterns[diagonal] = block * LOG2E
    far = jnp.full((side, side), rb_ref[N_BUCKETS - 1, head] * LOG2E, F32)
    for bi in range(near_ref.shape[1] // side):
        for bj in range(near_ref.shape[2] // side):
            near_ref[0, bi * side:(bi + 1) * side, bj * side:(bj + 1) * side] = patterns.get(bi - bj + 1, far)

    @pl.when(head == 0)
    def _():
        n_q = n_heads * ls

        def slots(shape):
            row = lax.broadcasted_iota(jnp.int32, shape, 0)
            lane = lax.broadcasted_iota(jnp.int32, shape, 1)
            valid = jnp.logical_and(_rem(row, n_heads) == _div(_rem(lane, n_q), ls), lane < 2 * n_q)
            return _div(row, n_heads), _rem(lane, ls), valid

        def paged(ref, offset):
            key, query, valid = slots(ref.shape)
            bucket = bucket_of(query + offset - key)
            bias = jnp.zeros(ref.shape, F32)
            for b in range(N_BUCKETS):
                bias = jnp.where(bucket == b, rbl_ref[b:b + 1, :], bias)
            ref[...] = jnp.where(valid, bias * LOG2E, NEG_INF)

        paged(new_ref, 0)
        paged(tail_ref, PAGE_SIZE)
        tile = (V7X_SUBLANES, V7X_LANES)
        _, _, valid = slots(tile)
        far = jnp.broadcast_to(rbl_ref[N_BUCKETS - 1:N_BUCKETS, :] * LOG2E, tile)
        far_ref[0:8] = jnp.where(valid, far, NEG_INF)
        far_ref[8:16] = jnp.where(valid, far, 0.0)
        far_ref[16:24] = jnp.where(valid, NEG_INF, 0.0)
        far_ref[24:32] = jnp.where(valid, 1.0, 0.0)


def _bias_tables(rel_bias, tq, ls):
    n_heads = rel_bias.shape[1]
    n_q = n_heads * ls
    assert n_heads == V7X_SUBLANES and 2 * n_q <= V7X_LANES
    near_w = tq + MAX_DISTANCE
    page_rows = PAGE_SIZE * n_heads
    new_rows = -(-n_q // V7X_LANES) * V7X_LANES
    lane_head = (jnp.arange(V7X_LANES) % n_q) // ls
    rb_lanes = jnp.where(jnp.arange(V7X_LANES) < 2 * n_q, rel_bias[:, lane_head], 0.0)
    return pl.pallas_call(
        functools.partial(_bias_tables_kernel, n_heads=n_heads, ls=ls),
        grid=(n_heads,),
        in_specs=[pl.BlockSpec(memory_space=pltpu.SMEM), _whole((N_BUCKETS, V7X_LANES), 1)],
        out_specs=[
            pl.BlockSpec((1, tq, near_w), lambda h: (h, 0, 0)),
            _whole((new_rows, V7X_LANES), 1), _whole((page_rows, V7X_LANES), 1),
            _whole((4 * V7X_SUBLANES, V7X_LANES), 1),
        ],
        out_shape=[
            jax.ShapeDtypeStruct((n_heads, tq, near_w), F32),
            jax.ShapeDtypeStruct((new_rows, V7X_LANES), F32),
            jax.ShapeDtypeStruct((page_rows, V7X_LANES), F32),
            jax.ShapeDtypeStruct((4 * V7X_SUBLANES, V7X_LANES), F32),
        ],
        compiler_params=_params("arbitrary"),
        name="bias_tables",
    )(rel_bias, rb_lanes)


def _gated_conv_chunk(xn, wu, wb, wc, wz, cw, t, halo0, halo1):
    u = _dot(xn, wu)
    gate_b = _dot(xn, wb)
    gate_c = _dot(xn, wc)
    z = _dot(xn, wz)
    cu = gate_c * u
    prev1 = jnp.where(t >= 1, pltpu.roll(cu, 1, 0), halo1)
    prev2 = jnp.where(t >= 2, pltpu.roll(cu, 2, 0), jnp.where(t == 1, halo1, halo0))
    conv = prev2 * cw[0:1] + prev1 * cw[1:2] + cu * cw[2:3]
    return cu, gate_b * conv * _silu(z)


def _conv_prompt_kernel(*refs, tiles_per_seq, from_input):
    if from_input:
        x_ref, meta_ref, *refs = refs
    else:
        h_ref, *refs = refs
    nw_ref, wu_ref, wb_ref, wc_ref, wz_ref, cw_ref, wo_ref, o_ref, tail_ref, xn_scr, carry_scr = refs
    i, j = pl.program_id(0), pl.program_id(1)
    tm, tn = o_ref.shape[0], wu_ref.shape[1]
    starts_seq = i % tiles_per_seq == 0

    def start_tile(h):
        xn_scr[...] = _rms_norm(h, nw_ref[...]).astype(BF16)
        o_ref[...] = h

    if from_input:
        @pl.when(jnp.logical_and(j == 0, starts_seq))
        def _():
            start_tile(jnp.concatenate([meta_ref[...], x_ref[0:tm - N_META, :]], axis=0))

        @pl.when(jnp.logical_and(j == 0, jnp.logical_not(starts_seq)))
        def _():
            start_tile(x_ref[...])
    else:
        @pl.when(j == 0)
        def _():
            start_tile(h_ref[...])

    @pl.when(jnp.logical_and(j == 0, starts_seq))
    def _():
        carry_scr[...] = jnp.zeros(carry_scr.shape, F32)

    t = lax.broadcasted_iota(jnp.int32, (tm, tn), 0)
    halo0 = carry_scr[j, V7X_SUBLANES - 2:V7X_SUBLANES - 1, :]
    halo1 = carry_scr[j, V7X_SUBLANES - 1:V7X_SUBLANES, :]
    cu, y = _gated_conv_chunk(xn_scr[...], wu_ref[...], wb_ref[...], wc_ref[...], wz_ref[...], cw_ref[...],
                              t, halo0, halo1)
    last_rows = cu[tm - V7X_SUBLANES:tm]
    carry_scr[j] = last_rows
    tail_ref[0] = last_rows
    o_ref[...] += _dot(y.astype(BF16), wo_ref[...])


def _conv_sample_kernel(h_ref, nw_ref, wu_ref, wb_ref, wc_ref, wz_ref, cw_ref, wo_ref, halo0_ref, halo1_ref,
                        o_ref, cu_ref, wu_out, wb_out, wc_out, wz_out, wo_out, xn_scr, *, seg_rows):
    j = pl.program_id(0)
    tm, tn = h_ref.shape[0], wu_ref.shape[1]

    @pl.when(j == 0)
    def _():
        h = h_ref[...]
        xn_scr[...] = _rms_norm(h, nw_ref[...]).astype(BF16)
        o_ref[...] = h

    wu, wb, wc, wz, wo = _to_bf16((wu_ref, wb_ref, wc_ref, wz_ref, wo_ref), (wu_out, wb_out, wc_out, wz_out, wo_out))
    t = _rem(lax.broadcasted_iota(jnp.int32, (tm, tn), 0), seg_rows)
    cu, y = _gated_conv_chunk(xn_scr[...], wu, wb, wc, wz, cw_ref[...], t, halo0_ref[...], halo1_ref[...])
    cu_ref[...] = cu
    o_ref[...] += _dot(y.astype(BF16), wo)


def _w_in_specs(layer, d, tn, col_map):
    n_chunks = d // tn
    return [
        pl.BlockSpec((None, d, tn), functools.partial(col_map, layer, c * n_chunks)) for c in range(4)
    ]


def _conv_layer_prompt(h, norm_w, w_in_groups, conv_w, w_out, layer, batch, seq_len):
    from_input = isinstance(h, tuple)
    d = w_out.shape[0]
    n_rows = batch * seq_len
    tm, tn = _row_tile(seq_len, ROW_TILE_CAP), PROMPT_COL_TILE
    tiles_per_seq = seq_len // tm
    n_tiles, n_chunks = n_rows // tm, d // tn
    if from_input:
        x_len = seq_len - N_META
        assert tm > N_META and tiles_per_seq * tm - N_META == x_len

        assert x_len % V7X_BF16_SUBLANES == 0 and N_META % V7X_BF16_SUBLANES == 0

        def x_window(i, j):
            row = (i // tiles_per_seq) * x_len + jnp.maximum((i % tiles_per_seq) * tm - N_META, 0)
            return pl.multiple_of(row, V7X_BF16_SUBLANES), 0

        h_specs = [pl.BlockSpec((pl.Element(tm), pl.Element(d)), x_window), _whole((N_META, d), 2)]
        h_args = list(h)
    else:
        h_specs = [pl.BlockSpec((tm, d), lambda i, j: (i, 0))]
        h_args = [h]
    w_chunk = pl.BlockSpec((d, tn), lambda i, j: (0, j))
    out, tails = pl.pallas_call(
        functools.partial(_conv_prompt_kernel, tiles_per_seq=tiles_per_seq, from_input=from_input),
        grid=(n_tiles, n_chunks),
        in_specs=[
            *h_specs,
            _whole((1, d), 2),
            w_chunk, w_chunk, w_chunk, w_chunk,
            pl.BlockSpec((None, CONV_WIDTH, tn), lambda i, j: (layer, 0, j)),
            pl.BlockSpec((tn, d), lambda i, j: (j, 0)),
        ],
        out_specs=[
            pl.BlockSpec((tm, d), lambda i, j: (i, 0)),
            pl.BlockSpec((1, V7X_SUBLANES, tn), lambda i, j: (i, 0, j)),
        ],
        out_shape=[
            jax.ShapeDtypeStruct((n_rows, d), F32),
            jax.ShapeDtypeStruct((n_tiles, V7X_SUBLANES, d), F32),
        ],
        scratch_shapes=[
            pltpu.VMEM((tm, d), BF16),
            pltpu.VMEM((n_chunks, V7X_SUBLANES, tn), F32),
        ],
        compiler_params=_params("arbitrary", "arbitrary"),
        name="conv_mixer_prompt",
    )(*h_args, norm_w, *w_in_groups, conv_w, w_out)
    state = tails.reshape(batch, tiles_per_seq, V7X_SUBLANES, d)[:, -1, V7X_SUBLANES - (CONV_WIDTH - 1):]
    return out, state


def _conv_layer_sample(h, norm_w, w_in, conv_w, w_out, layer, state, seq_len):
    n_rows, d = h.shape
    tn = COL_TILE
    n_chunks = d // tn
    n_seq = n_rows // seq_len
    halo0 = jnp.repeat(state[:, 0], seq_len, axis=0)
    halo1 = jnp.repeat(state[:, 1], seq_len, axis=0)
    rows_chunk = pl.BlockSpec((n_rows, tn), lambda j: (0, j))
    w_chunk = pl.BlockSpec((d, tn), lambda j: (0, j))
    w_bf16 = jax.ShapeDtypeStruct((d, d), BF16)
    out, cu, *weights = pl.pallas_call(
        functools.partial(_conv_sample_kernel, seg_rows=seq_len),
        grid=(n_chunks,),
        in_specs=[
            _whole((n_rows, d), 1),
            _whole((1, d), 1),
            *_w_in_specs(layer, d, tn, lambda l, c0, j: (l, 0, c0 + j)),
            pl.BlockSpec((None, CONV_WIDTH, tn), lambda j: (layer, 0, j)),
            pl.BlockSpec((None, tn, d), lambda j: (layer, j, 0)),
            rows_chunk, rows_chunk,
        ],
        out_specs=[
            _whole((n_rows, d), 1), rows_chunk,
            w_chunk, w_chunk, w_chunk, w_chunk,
            pl.BlockSpec((tn, d), lambda j: (j, 0)),
        ],
        out_shape=[
            jax.ShapeDtypeStruct((n_rows, d), F32),
            jax.ShapeDtypeStruct((n_rows, d), F32),
            w_bf16, w_bf16, w_bf16, w_bf16, w_bf16,
        ],
        scratch_shapes=[pltpu.VMEM((n_rows, d), BF16)],
        compiler_params=_params("arbitrary"),
        name="conv_mixer_sample",
    )(h, norm_w, w_in, w_in, w_in, w_in, conv_w, w_out, halo0, halo1)
    new_state = cu.reshape(n_seq, seq_len, d)[:, seq_len - (CONV_WIDTH - 1):]
    return out, new_state, weights[:4], weights[4]


def _attn_inproj_sample_kernel(h_ref, nw_ref, wq_ref, wk_ref, wv_ref, wz_ref,
                               q_ref, k_ref, v_ref, z_ref, wq_out, wk_out, wv_out, wz_out, xn_scr):
    @pl.when(pl.program_id(0) == 0)
    def _():
        xn_scr[...] = _rms_norm(h_ref[...], nw_ref[...]).astype(BF16)

    wq, wk, wv, wz = _to_bf16((wq_ref, wk_ref, wv_ref, wz_ref), (wq_out, wk_out, wv_out, wz_out))
    xn = xn_scr[...]
    q_ref[...] = _dot(xn, wq) * SCORE_SCALE
    k_ref[...] = _dot(xn, wk)
    v_ref[...] = _dot(xn, wv)
    z_ref[...] = _dot(xn, wz)


def _attn_inproj_sample(h, norm_w, w_in, layer):
    n_rows, d = h.shape
    tn = COL_TILE
    chunk = pl.BlockSpec((n_rows, tn), lambda j: (0, j))
    w_chunk = pl.BlockSpec((d, tn), lambda j: (0, j))
    outs = pl.pallas_call(
        _attn_inproj_sample_kernel,
        grid=(d // tn,),
        in_specs=[_whole((n_rows, d), 1), _whole((1, d), 1),
                  *_w_in_specs(layer, d, tn, lambda l, c0, j: (l, 0, c0 + j))],
        out_specs=[chunk] * 4 + [w_chunk] * 4,
        out_shape=[jax.ShapeDtypeStruct((n_rows, d), F32)] * 4 + [jax.ShapeDtypeStruct((d, d), BF16)] * 4,
        scratch_shapes=[pltpu.VMEM((n_rows, d), BF16)],
        compiler_params=_params("arbitrary"),
        name="attn_inproj_sample",
    )(h, norm_w, w_in, w_in, w_in, w_in)
    return outs[:4], outs[4:]


def _attn_inproj_prompt_kernel(h_ref, nw_ref, wq_ref, wk_ref, wv_ref, wz_ref, *rest, n_earlier):
    earlier, rest = rest[:2 * n_earlier], rest[2 * n_earlier:]
    q_ref, kb_ref, vb_ref, z_ref, k_out, v_out, xn_scr, chunk_buf, chunk_sem, *staging = rest
    i, j = pl.program_id(0), pl.program_id(1)
    n_tiles, n_chunks = pl.num_programs(0), pl.num_programs(1)
    tm = chunk_buf.shape[2]
    step = i * n_chunks + j
    last_step = step == n_tiles * n_chunks - 1
    slot = lax.rem(step, 2)

    def chunk_copies(at_step, buf_slot):
        rows = pl.ds(pl.multiple_of((at_step // n_chunks) * tm, V7X_SUBLANES), tm)
        heads_per_chunk = chunk_buf.shape[3] // HEAD_WIDTH
        for c, out in enumerate((k_out, v_out)):
            for hh in range(heads_per_chunk):
                head = lax.rem(at_step, n_chunks) * heads_per_chunk + hh
                src = chunk_buf.at[c, buf_slot, :, hh * HEAD_WIDTH:(hh + 1) * HEAD_WIDTH]
                dst = out.at[n_earlier, rows, head, :] if n_earlier else out.at[rows, head, :]
                yield pltpu.make_async_copy(src, dst, chunk_sem.at[c, buf_slot, hh])

    def slab_copies(slab, inward):
        stage, in_sem, out_sem = staging
        slab_rows = stage.shape[2]
        rows = pl.ds(slab * slab_rows, slab_rows)
        slab_slot = lax.rem(slab, 2)
        for a, src in enumerate(earlier):
            buf = stage.at[a, slab_slot]
            if inward:
                yield pltpu.make_async_copy(src.at[rows], buf, in_sem.at[a, slab_slot])
            else:
                yield pltpu.make_async_copy(buf, (k_out, v_out)[a % 2].at[a // 2, rows], out_sem.at[a, slab_slot])

    def start(copies):
        for copy in copies:
            copy.start()

    def wait(copies):
        for copy in copies:
            copy.wait()

    if n_earlier:
        @pl.when(step == 0)
        def _():
            start(slab_copies(0, inward=True))

        @pl.when(step == 1)
        def _():
            wait(slab_copies(0, inward=True))
            start(slab_copies(0, inward=False))
            start(slab_copies(1, inward=True))

    @pl.when(step >= 2)
    def _():
        wait(chunk_copies(step - 2, slot))
        if n_earlier:
            wait(slab_copies(step - 2, inward=False))
            wait(slab_copies(step - 1, inward=True))
            start(slab_copies(step - 1, inward=False))
            start(slab_copies(step, inward=True))

    @pl.when(j == 0)
    def _():
        xn_scr[...] = _rms_norm(h_ref[...], nw_ref[...]).astype(BF16)

    xn = xn_scr[...]
    q_ref[...] = (_dot(xn, wq_ref[...]) * SCORE_SCALE).astype(BF16)
    z_ref[...] = _dot(xn, wz_ref[...]).astype(BF16)
    k = _dot(xn, wk_ref[...])
    v = _dot(xn, wv_ref[...])
    kb_ref[...] = k.astype(BF16)
    vb_ref[...] = v.astype(BF16)
    chunk_buf[0, slot] = k
    chunk_buf[1, slot] = v
    start(chunk_copies(step, slot))

    @pl.when(last_step)
    def _():
        wait(chunk_copies(step, slot))
        wait(chunk_copies(step - 1, 1 - slot))
        if n_earlier:
            wait(slab_copies(step, inward=True))
            start(slab_copies(step, inward=False))
            wait(slab_copies(step - 1, inward=False))
            wait(slab_copies(step, inward=False))


def _attn_inproj_prompt(h, norm_w, weights, tm, earlier_kv=()):
    n_rows, d = h.shape
    tn = PROMPT_COL_TILE
    assert tn % HEAD_WIDTH == 0
    n_heads = d // HEAD_WIDTH
    n_earlier = len(earlier_kv)
    n_steps = (n_rows // tm) * (d // tn)
    assert n_steps >= 2 and n_rows % n_steps == 0
    chunk = pl.BlockSpec((tm, tn), lambda i, j: (i, j))
    w_chunk = pl.BlockSpec((d, tn), lambda i, j: (0, j))
    in_hbm = pl.BlockSpec(memory_space=pl.ANY)
    act = jax.ShapeDtypeStruct((n_rows, d), BF16)
    kv_shape = (n_rows, n_heads, HEAD_WIDTH)
    scratch = [pltpu.VMEM((tm, d), BF16), pltpu.VMEM((2, 2, tm, tn), F32),
               pltpu.SemaphoreType.DMA((2, 2, tn // HEAD_WIDTH))]
    if n_earlier:
        kv_shape = (n_earlier + 1, *kv_shape)
        scratch += [
            pltpu.VMEM((2 * n_earlier, 2, n_rows // n_steps, n_heads, HEAD_WIDTH), F32),
            pltpu.SemaphoreType.DMA((2 * n_earlier, 2)),
            pltpu.SemaphoreType.DMA((2 * n_earlier, 2)),
        ]
    kv = jax.ShapeDtypeStruct(kv_shape, F32)
    return pl.pallas_call(
        functools.partial(_attn_inproj_prompt_kernel, n_earlier=n_earlier),
        grid=(n_rows // tm, d // tn),
        in_specs=[pl.BlockSpec((tm, d), lambda i, j: (i, 0)), _whole((1, d), 2), w_chunk, w_chunk, w_chunk, w_chunk]
        + [in_hbm] * (2 * n_earlier),
        out_specs=[chunk, chunk, chunk, chunk, in_hbm, in_hbm],
        out_shape=[act, act, act, act, kv, kv],
        scratch_shapes=scratch,
        compiler_params=_params("arbitrary", "arbitrary"),
        name="attn_inproj_prompt",
    )(h, norm_w, *weights, *[a for pair in earlier_kv for a in pair])


def _prompt_attn_kernel(rb_ref, q_ref, k_ref, v_ref, z_ref, near_ref, lam_ref, sw_ref,
                        y_ref, kb_scr, vb_scr, s_scr, a_scr, *, lam_init):
    head = pl.program_id(1)
    seq_len = q_ref.shape[1]
    padded_len = kb_scr.shape[0]
    tq = near_ref.shape[1]
    kb_scr[0:seq_len] = k_ref[0]
    vb_scr[0:seq_len] = v_ref[0]
    if padded_len > seq_len:
        kb_scr[seq_len:padded_len] = jnp.zeros((padded_len - seq_len, HEAD_WIDTH), BF16)
        vb_scr[seq_len:padded_len] = jnp.zeros((padded_len - seq_len, HEAD_WIDTH), BF16)
    lam = _lambda(lam_ref[...], lam_init)
    far_bias = rb_ref[N_BUCKETS - 1, head] * LOG2E

    for r0 in range(0, seq_len, tq):
        rows = min(tq, seq_len - r0)
        near_lo = max(r0 - MAX_DISTANCE, 0)
        near_hi = r0 + tq
        near_w = near_hi - near_lo
        col0 = near_lo - (r0 - MAX_DISTANCE)
        q = q_ref[0, r0:r0 + rows, :]
        row = lax.broadcasted_iota(jnp.int32, (rows, near_w), 0)
        col = lax.broadcasted_iota(jnp.int32, (rows, near_w), 1)
        visible = col + col0 <= row + MAX_DISTANCE
        near_bias = near_ref[0, 0:rows, col0:col0 + near_w]
        shifts, denom = [], []
        for c in range(2):
            lanes = slice(c * HEAD_DIM, (c + 1) * HEAD_DIM)
            s_near = _dot_nt(q[:, lanes], kb_scr[near_lo:near_hi, lanes])
            s_near = jnp.where(visible, s_near + near_bias, NEG_INF)
            s_scr[c, 0:rows, near_lo:near_hi] = s_near
            m = jnp.max(s_near, axis=-1, keepdims=True)
            if near_lo > 0:
                s_far = _dot_nt(q[:, lanes], kb_scr[0:near_lo, lanes])
                s_scr[c, 0:rows, 0:near_lo] = s_far
                m = jnp.maximum(m, jnp.max(s_far, axis=-1, keepdims=True) + far_bias)
            shifts.append(m)
        for c in range(2):
            m = shifts[c]
            e = jnp.exp2(s_scr[c, 0:rows, near_lo:near_hi] - m)
            total = jnp.sum(e, axis=-1, keepdims=True)
            s_scr[c, 0:rows, near_lo:near_hi] = e
            if near_lo > 0:
                e = jnp.exp2(s_scr[c, 0:rows, 0:near_lo] - (m - far_bias))
                total = total + jnp.sum(e, axis=-1, keepdims=True)
                s_scr[c, 0:rows, 0:near_lo] = e
            denom.append(total)
        w1 = 1.0 / denom[0]
        w2 = lam / denom[1]
        a_scr[0:rows, 0:near_hi] = (s_scr[0, 0:rows, 0:near_hi] * w1 - s_scr[1, 0:rows, 0:near_hi] * w2).astype(BF16)
        o = _dot(a_scr[0:rows, 0:near_hi], vb_scr[0:near_hi, :])
        z = z_ref[0, r0:r0 + rows, :].astype(F32)
        y_ref[0, r0:r0 + rows, :] = _sub_norm_gate(o, sw_ref[...], lam_init, z).astype(y_ref.dtype)


def _prompt_attention(q, k, v, z, rel_bias, near_table, lam_params, sub_w, lam_init):
    batch, seq_len, d = q.shape
    n_heads = d // HEAD_WIDTH
    tq = near_table.shape[1]
    padded_len = -(-seq_len // tq) * tq
    head_cols = pl.BlockSpec((1, seq_len, HEAD_WIDTH), lambda b, h: (b, 0, h))
    return pl.pallas_call(
        functools.partial(_prompt_attn_kernel, lam_init=lam_init),
        grid=(batch, n_heads),
        in_specs=[
            pl.BlockSpec(memory_space=pltpu.SMEM),
            head_cols, head_cols, head_cols, head_cols,
            pl.BlockSpec((1, tq, tq + MAX_DISTANCE), lambda b, h: (h, 0, 0)),
            _whole((4, HEAD_DIM), 2),
            _whole((1, HEAD_WIDTH), 2),
        ],
        out_specs=head_cols,
        out_shape=jax.ShapeDtypeStruct((batch, seq_len, d), BF16),
        scratch_shapes=[
            pltpu.VMEM((padded_len, HEAD_WIDTH), BF16),
            pltpu.VMEM((padded_len, HEAD_WIDTH), BF16),
            pltpu.VMEM((2, tq, padded_len), F32),
            pltpu.VMEM((tq, padded_len), BF16),
        ],
        compiler_params=_params("arbitrary", "arbitrary"),
        name="prompt_attention",
    )(rel_bias, q, k, v, z, near_table, lam_params, sub_w)


def _decode_attn_kernel(pt_ref, q_ref, kn_ref, vn_ref, z_ref, *rest, n_pg, lam_init):
    k_pages, v_pages = rest[:n_pg], rest[n_pg:2 * n_pg]
    (new_ref, tail_ref, far_ref, lam_ref, sw_ref, y_ref,
     s_scr, m_seen_scr, m_used_scr, l_scr, acc_scr) = rest[2 * n_pg:]
    del pt_ref
    step, n_groups = pl.program_id(1), pl.num_programs(1) - 1
    ls, d = z_ref.shape
    n_heads = d // HEAD_WIDTH
    n_q = n_heads * ls
    page_rows = PAGE_SIZE * n_heads
    tile = (V7X_SUBLANES, V7X_LANES)
    q_cols = q_ref[0]
    far_bias, far_bias_or_0, m_start, valid = far_ref[0:8], far_ref[8:16], far_ref[16:24], far_ref[24:32]
    slot = lax.rem(step, 2)

    def per_tile(x):
        return x.reshape(x.shape[0] // V7X_SUBLANES, *tile)

    def lanes_to_rows(stat):
        per_lane = jnp.sum(stat * valid, axis=0, keepdims=True)
        return jnp.broadcast_to(per_lane, tile).T[:, 0:1]

    def fold(probs, values, m_from, m_to):
        alpha = jnp.exp2(m_from - m_to)
        total = alpha * l_scr[...]
        pv = None
        for p, vb in zip(probs, values):
            total = total + jnp.sum(p, axis=0)
            part = lax.dot_general(p.reshape(vb.shape[0], V7X_LANES), vb, (((0,), (0,)), ((), ())),
                                   preferred_element_type=F32)
            pv = part if pv is None else pv + part
        l_scr[...] = total
        acc_scr[...] = lanes_to_rows(alpha) * acc_scr[...] + pv

    @pl.when(step == 0)
    def _():
        m_seen_scr[...] = m_start
        m_used_scr[...] = m_start
        l_scr[...] = jnp.zeros(l_scr.shape, F32)
        acc_scr[...] = jnp.zeros(acc_scr.shape, F32)
        s_scr[1] = jnp.full(s_scr.shape[1:], NEG_INF, F32)

    keys_are_tail = step >= n_groups - 1
    values_are_tail = step == n_groups

    m_use = m_seen_scr[...]
    prev = s_scr.at[1 - slot]
    probs = []
    for r in range(n_pg):
        s = per_tile(prev[r * page_rows:(r + 1) * page_rows])
        if r == n_pg - 1:
            probs.append(jnp.exp2(s + jnp.where(values_are_tail, per_tile(tail_ref[...]), far_bias) - m_use))
        else:
            probs.append(jnp.exp2(s + (far_bias - m_use)))
    fold(probs, [ref[0, 0].reshape(page_rows, HEAD_WIDTH) for ref in v_pages], m_used_scr[...], m_use)
    m_used_scr[...] = m_use

    m_new = m_use
    for r, ref in enumerate(k_pages):
        logits = _dot(ref[0, 0].reshape(page_rows, HEAD_WIDTH), q_cols)
        s_scr[slot, r * page_rows:(r + 1) * page_rows] = logits
        s = per_tile(logits)
        if r == n_pg - 1:
            s = s + jnp.where(keys_are_tail, per_tile(tail_ref[...]), far_bias_or_0)
            m_new = jnp.maximum(m_new, jnp.max(s, axis=0))
        else:
            m_new = jnp.maximum(m_new, jnp.max(s, axis=0) + far_bias_or_0)
    m_seen_scr[...] = m_new

    @pl.when(step == n_groups)
    def _():
        shape = new_ref.shape
        key = _div(lax.broadcasted_iota(jnp.int32, shape, 0), n_heads)
        query = _rem(lax.broadcasted_iota(jnp.int32, shape, 1), ls)
        s = per_tile(_dot(kn_ref[0], q_cols) + jnp.where(key <= query, new_ref[...], NEG_INF))
        m_all = jnp.maximum(m_use, jnp.max(s, axis=0))
        fold([jnp.exp2(s - m_all)], [vn_ref[0]], m_use, m_all)
        lam = _lambda(lam_ref[...], lam_init)
        o = acc_scr[...] / lanes_to_rows(l_scr[...])
        o = o[0:n_q] - lam * o[n_q:2 * n_q]
        o = o * lax.rsqrt(jnp.mean(o * o, axis=-1, keepdims=True) + EPS)
        o = o * sw_ref[...] * (1.0 - lam_init)
        for h in range(n_heads):
            cols = slice(h * HEAD_WIDTH, (h + 1) * HEAD_WIDTH)
            y_ref[:, cols] = o[h * ls:(h + 1) * ls] * _silu(z_ref[:, cols])


def _decode_attention(q, k_new, v_new, z, cache_k, cache_v, page_table, layer, new_table, tail_table, far_table,
                      lam_params, sub_w, lam_init, ls):
    n_rows, d = q.shape
    batch, n_pages = page_table.shape
    n_heads = d // HEAD_WIDTH
    n_q = n_heads * ls
    new_rows = new_table.shape[0]
    n_pg = math.gcd(n_pages, PAGES_PER_STEP)
    n_groups = n_pages // n_pg
    q_t = q.reshape(batch, ls, n_heads, 2, HEAD_DIM).transpose(0, 3, 4, 2, 1).reshape(batch, 2, HEAD_DIM, n_q)
    q_cols = jnp.concatenate([
        jnp.pad(q_t[:, c], ((0, 0), (0, 0), (c * n_q, V7X_LANES - (c + 1) * n_q))) for c in range(2)], axis=1)
    pad = ((0, 0), (0, new_rows - n_q), (0, 0))
    k_rows = jnp.pad(k_new.reshape(batch, n_q, HEAD_WIDTH), pad)
    v_rows = jnp.pad(v_new.reshape(batch, n_q, HEAD_WIDTH), pad)

    def page_specs(group_of_step):
        return [
            pl.BlockSpec((1, 1, PAGE_SIZE, n_heads, HEAD_WIDTH), functools.partial(
                lambda r, b, s, pt: (layer, pt[b, group_of_step(s) * n_pg + r], 0, 0, 0), r))
            for r in range(n_pg)
        ]

    new_page = pl.BlockSpec((1, new_rows, HEAD_WIDTH), lambda b, s, pt: (b, 0, 0))
    grid_spec = pltpu.PrefetchScalarGridSpec(
        num_scalar_prefetch=1,
        grid=(batch, n_groups + 1),
        in_specs=[
            pl.BlockSpec((1, HEAD_WIDTH, V7X_LANES), lambda b, s, pt: (b, 0, 0)),
            new_page, new_page,
            pl.BlockSpec((ls, d), lambda b, s, pt: (b, 0)),
            *page_specs(lambda s: jnp.minimum(s, n_groups - 1)), *page_specs(lambda s: jnp.maximum(s - 1, 0)),
            _whole(new_table.shape, 3), _whole(tail_table.shape, 3), _whole(far_table.shape, 3),
            _whole((4, HEAD_DIM), 3), _whole((1, HEAD_WIDTH), 3),
        ],
        out_specs=pl.BlockSpec((ls, d), lambda b, s, pt: (b, 0)),
        scratch_shapes=[
            pltpu.VMEM((2, n_pg * PAGE_SIZE * n_heads, V7X_LANES), F32),
            pltpu.VMEM((V7X_SUBLANES, V7X_LANES), F32),
            pltpu.VMEM((V7X_SUBLANES, V7X_LANES), F32),
            pltpu.VMEM((V7X_SUBLANES, V7X_LANES), F32),
            pltpu.VMEM((V7X_LANES, HEAD_WIDTH), F32),
        ],
    )
    return pl.pallas_call(
        functools.partial(_decode_attn_kernel, n_pg=n_pg, lam_init=lam_init),
        grid_spec=grid_spec,
        out_shape=jax.ShapeDtypeStruct((n_rows, d), F32),
        compiler_params=_params("arbitrary", "arbitrary"),
        name="decode_attention",
    )(page_table, q_cols, k_rows, v_rows, z, *([cache_k] * n_pg), *([cache_v] * n_pg),
      new_table, tail_table, far_table, lam_params, sub_w)


def _outproj_kernel(y_ref, w_ref, h_ref, *rest, final_norm):
    h = h_ref[...] + _dot(y_ref[...], w_ref[...])
    if final_norm:
        fw_ref, o_ref = rest
        o_ref[...] = _rms_norm(h, fw_ref[...])
    else:
        (o_ref,) = rest
        o_ref[...] = h


def _outproj_prompt(y, w_out, h, batch, seq_len, final_w=None):
    n_rows, d = h.shape
    w_spec = _whole((d, d), 1)
    if final_w is None:
        tm = _row_tile(seq_len, ROW_TILE_CAP)
        rows = pl.BlockSpec((tm, d), lambda i: (i, 0))
        return pl.pallas_call(
            functools.partial(_outproj_kernel, final_norm=False),
            grid=(n_rows // tm,),
            in_specs=[rows, w_spec, rows],
            out_specs=rows,
            out_shape=jax.ShapeDtypeStruct((n_rows, d), F32),
            compiler_params=_params("arbitrary"),
            name="attn_outproj",
        )(y, w_out, h)
    out_len = seq_len - N_META
    tm = _row_tile(out_len, ROW_TILE_CAP)
    tiles_per_seq = out_len // tm
    assert seq_len % V7X_BF16_SUBLANES == 0 and N_META % V7X_BF16_SUBLANES == 0

    def rows_window(i):
        row = (i // tiles_per_seq) * seq_len + N_META + (i % tiles_per_seq) * tm
        return pl.multiple_of(row, V7X_BF16_SUBLANES), 0

    window = pl.BlockSpec((pl.Element(tm), pl.Element(d)), rows_window)
    return pl.pallas_call(
        functools.partial(_outproj_kernel, final_norm=True),
        grid=(batch * tiles_per_seq,),
        in_specs=[window, w_spec, window, _whole((1, d), 1)],
        out_specs=pl.BlockSpec((tm, d), lambda i: (i, 0)),
        out_shape=jax.ShapeDtypeStruct((batch * out_len, d), F32),
        compiler_params=_params("arbitrary"),
        name="attn_outproj_final",
    )(y, w_out, h, final_w)


def _outproj_sample_kernel(y_ref, w_ref, h_ref, o_ref, w_out):
    (w,) = _to_bf16((w_ref,), (w_out,))
    o_ref[...] = h_ref[...] + _dot(y_ref[...].astype(BF16), w)


def _outproj_sample(y, w_out, layer, h):
    n_rows, d = h.shape
    tn = OUT_COL_TILE
    cols = pl.BlockSpec((n_rows, tn), lambda j: (0, j))
    return pl.pallas_call(
        _outproj_sample_kernel,
        grid=(d // tn,),
        in_specs=[_whole((n_rows, d), 1), pl.BlockSpec((None, d, tn), lambda j: (layer, 0, j)), cols],
        out_specs=[cols, pl.BlockSpec((d, tn), lambda j: (0, j))],
        out_shape=[jax.ShapeDtypeStruct((n_rows, d), F32), jax.ShapeDtypeStruct((d, d), BF16)],
        compiler_params=_params("arbitrary"),
        name="attn_outproj_sample",
    )(y, w_out, h)


def _final_norm_kernel(h_ref, w_ref, o_ref):
    o_ref[...] = _rms_norm(h_ref[...], w_ref[...])


def _final_norm(h, w):
    return pl.pallas_call(
        _final_norm_kernel, out_shape=jax.ShapeDtypeStruct(h.shape, F32), name="final_norm_sample")(h, w)


def kernel(x_prompt, x_sample, state_conv, cache_k, cache_v, page_table, meta_tokens, rel_bias, norm_w,
           final_norm_w, conv_w_in, conv_w, conv_w_out, attn_w_in, attn_lambda, attn_subln_w, attn_w_out):
    b, seq, d = x_prompt.shape
    db, ls, _ = x_sample.shape
    depth = norm_w.shape[0]
    lp = N_META + seq
    n_heads = d // HEAD_WIDTH
    assert d % HEAD_WIDTH == 0 and d % COL_TILE == 0 and d % OUT_COL_TILE == 0
    assert ls >= CONV_WIDTH - 1 and (db * ls) % V7X_BF16_SUBLANES == 0
    assert cache_k.shape[2:] == (PAGE_SIZE, n_heads, HEAD_WIDTH) and cache_v.shape == cache_k.shape
    assert ls <= PAGE_SIZE
    assert depth % 2 == 0, "the final norm is fused into the last attention mixer's out-projection"

    hp = (x_prompt.reshape(b * seq, d), meta_tokens.astype(x_prompt.dtype))
    hs = x_sample.reshape(db * ls, d)
    tm_p = _row_tile(lp, ROW_TILE_CAP)
    fw = final_norm_w.reshape(1, d)

    near_table, new_table, tail_table, far_table = _bias_tables(rel_bias, Q_TILE, ls)

    conv_p, conv_s, k_s, v_s = [], [], [], []
    kv_p = []
    for i in range(depth):
        nw = norm_w[i].reshape(1, d)
        j = i // 2
        last = i == depth - 1
        if i % 2 == 0:
            hs, st_s, w_in_b, w_out_b = _conv_layer_sample(hs, nw, conv_w_in, conv_w, conv_w_out, j, state_conv[j], ls)
            hp, st_p = _conv_layer_prompt(hp, nw, w_in_b, conv_w, w_out_b, j, b, lp)
            conv_p.append(st_p)
            conv_s.append(st_s)
        else:
            lam_init = 0.8 - 0.6 * math.exp(-0.3 * i)
            sub_w = attn_subln_w[j].reshape(1, HEAD_WIDTH)
            (qs, ks, vs, zs), w_in_b = _attn_inproj_sample(hs, nw, attn_w_in, j)
            ys = _decode_attention(qs, ks, vs, zs, cache_k, cache_v, page_table, j, new_table, tail_table,
                                   far_table, attn_lambda[j], sub_w, lam_init, ls)
            hs, w_out_b = _outproj_sample(ys, attn_w_out, j, hs)
            last_attn = j == depth // 2 - 1
            q, k, v, z, k_out, v_out = _attn_inproj_prompt(hp, nw, w_in_b, tm_p, kv_p if last_attn else ())
            kv_p = [(k_out, v_out)] if last_attn else kv_p + [(k_out, v_out)]
            yp = _prompt_attention(q.reshape(b, lp, d), k.reshape(b, lp, d), v.reshape(b, lp, d),
                                   z.reshape(b, lp, d), rel_bias, near_table, attn_lambda[j], sub_w, lam_init)
            hp = _outproj_prompt(yp.reshape(b * lp, d), w_out_b, hp, b, lp, fw if last else None)
            k_s.append(ks.reshape(db, ls, n_heads, HEAD_WIDTH))
            v_s.append(vs.reshape(db, ls, n_heads, HEAD_WIDTH))

    y_prompt = hp.reshape(b, seq, d)
    y_sample = _final_norm(hs, fw).reshape(db, ls, d)
    k_p, v_p = (a.reshape(depth // 2, b, lp, n_heads, HEAD_WIDTH) for a in kv_p[0])
    return (y_prompt, y_sample, jnp.stack(conv_p), jnp.stack(conv_s), k_p, v_p, jnp.stack(k_s), jnp.stack(v_s))
```

```python
import functools
import math

import jax
import jax.numpy as jnp
from jax import lax
from jax.experimental import pallas as pl
from jax.experimental.pallas import tpu as pltpu

N_META = 16
HEAD_DIM = 128
HEAD_WIDTH = 2 * HEAD_DIM
PAGE_SIZE = 128
N_BUCKETS = 32
MAX_EXACT = N_BUCKETS // 2
MAX_DISTANCE = 128
CONV_WIDTH = 3
EPS = 1e-6
NEG_INF = -1e30
LOG2E = math.log2(math.e)
SCORE_SCALE = HEAD_DIM ** -0.5 * LOG2E

BF16 = jnp.bfloat16
F32 = jnp.float32

V7X_LANES = 128
V7X_SUBLANES = 8
V7X_BF16_SUBLANES = 16
V7X_VMEM_LIMIT_BYTES = 56 * 1024 * 1024

ROW_TILE_CAP = 768
COL_TILE = 256
PROMPT_COL_TILE = 512
OUT_COL_TILE = 512
Q_TILE = 512
PAGES_PER_STEP = 8


def _row_tile(n_rows, cap):
    best = None
    for t in range(V7X_BF16_SUBLANES, min(n_rows, cap) + 1, V7X_BF16_SUBLANES):
        if n_rows % t == 0:
            best = t
    return n_rows if best is None else best


def _params(*semantics):
    return pltpu.CompilerParams(dimension_semantics=semantics, vmem_limit_bytes=V7X_VMEM_LIMIT_BYTES)


def _dot(a, b):
    return jnp.dot(a, b, preferred_element_type=F32)


def _dot_nt(a, b):
    return lax.dot_general(a, b, (((1,), (1,)), ((), ())), preferred_element_type=F32)


def _rms_norm(x, w):
    return x * lax.rsqrt(jnp.mean(x * x, axis=-1, keepdims=True) + EPS) * w


def _silu(z):
    return z * (1.0 / (1.0 + jnp.exp(-z)))


def _lambda(lp, lam_init):
    a = jnp.sum(lp[0:1] * lp[1:2], axis=-1, keepdims=True)
    b = jnp.sum(lp[2:3] * lp[3:4], axis=-1, keepdims=True)
    return jnp.exp(a) - jnp.exp(b) + lam_init


def _sub_norm_gate(o, sub_w, lam_init, z):
    o = o * lax.rsqrt(jnp.mean(o * o, axis=-1, keepdims=True) + EPS)
    o = o * sub_w * (1.0 - lam_init)
    return o * _silu(z)


def _div(x, n):
    return x >> (n.bit_length() - 1) if n & (n - 1) == 0 else lax.div(x, jnp.int32(n))


def _rem(x, n):
    return x & (n - 1) if n & (n - 1) == 0 else lax.rem(x, jnp.int32(n))


def _to_bf16(src_refs, dst_refs):
    out = []
    for src, dst in zip(src_refs, dst_refs):
        w = src[...].astype(BF16)
        dst[...] = w
        out.append(w)
    return out


def _whole(shape, n_grid_axes):
    return pl.BlockSpec(shape, lambda *_: (0,) * len(shape))


def _bias_tables_kernel(rb_ref, rbl_ref, near_ref, new_ref, tail_ref, far_ref, *, n_heads, ls):
    head = pl.program_id(0)

    def bucket_of(n):
        n = jnp.maximum(n, 0)
        nf = jnp.maximum(n, 1).astype(F32)
        large = MAX_EXACT + (
            jnp.log(nf / MAX_EXACT) / math.log(MAX_DISTANCE / MAX_EXACT) * (N_BUCKETS - MAX_EXACT)
        ).astype(jnp.int32)
        large = jnp.minimum(large, N_BUCKETS - 1)
        return jnp.where(n < MAX_EXACT, n, large)

    side = MAX_DISTANCE
    delta = lax.broadcasted_iota(jnp.int32, (side, side), 0) - lax.broadcasted_iota(jnp.int32, (side, side), 1)
    patterns = {}
    for diagonal in (0, 1):
        bucket = bucket_of(delta + diagonal * side)
        block = jnp.zeros((side, side), F32)
        for b in range(N_BUCKETS):
            block = jnp.where(bucket == b, rb_ref[b, head], block)
        patterns[diagonal] = block * LOG2E
    far = jnp.full((side, side), rb_ref[N_BUCKETS - 1, head] * LOG2E, F32)
    for bi in range(near_ref.shape[1] // side):
        for bj in range(near_ref.shape[2] // side):
            near_ref[0, bi * side:(bi + 1) * side, bj * side:(bj + 1) * side] = patterns.get(bi - bj + 1, far)

    @pl.when(head == 0)
    def _():
        n_q = n_heads * ls

        def slots(shape):
            row = lax.broadcasted_iota(jnp.int32, shape, 0)
            lane = lax.broadcasted_iota(jnp.int32, shape, 1)
            valid = jnp.logical_and(_rem(row, n_heads) == _div(_rem(lane, n_q), ls), lane < 2 * n_q)
            return _div(row, n_heads), _rem(lane, ls), valid

        def paged(ref, offset):
            key, query, valid = slots(ref.shape)
            bucket = bucket_of(query + offset - key)
            bias = jnp.zeros(ref.shape, F32)
            for b in range(N_BUCKETS):
                bias = jnp.where(bucket == b, rbl_ref[b:b + 1, :], bias)
            ref[...] = jnp.where(valid, bias * LOG2E, NEG_INF)

        paged(new_ref, 0)
        paged(tail_ref, PAGE_SIZE)
        tile = (V7X_SUBLANES, V7X_LANES)
        _, _, valid = slots(tile)
        far = jnp.broadcast_to(rbl_ref[N_BUCKETS - 1:N_BUCKETS, :] * LOG2E, tile)
        far_ref[0:8] = jnp.where(valid, far, NEG_INF)
        far_ref[8:16] = jnp.where(valid, far, 0.0)
        far_ref[16:24] = jnp.where(valid, NEG_INF, 0.0)
        far_ref[24:32] = jnp.where(valid, 1.0, 0.0)


def _bias_tables(rel_bias, tq, ls):
    n_heads = rel_bias.shape[1]
    n_q = n_heads * ls
    assert n_heads == V7X_SUBLANES and 2 * n_q <= V7X_LANES
    near_w = tq + MAX_DISTANCE
    page_rows = PAGE_SIZE * n_heads
    new_rows = -(-n_q // V7X_LANES) * V7X_LANES
    lane_head = (jnp.arange(V7X_LANES) % n_q) // ls
    rb_lanes = jnp.where(jnp.arange(V7X_LANES) < 2 * n_q, rel_bias[:, lane_head], 0.0)
    return pl.pallas_call(
        functools.partial(_bias_tables_kernel, n_heads=n_heads, ls=ls),
        grid=(n_heads,),
        in_specs=[pl.BlockSpec(memory_space=pltpu.SMEM), _whole((N_BUCKETS, V7X_LANES), 1)],
        out_specs=[
            pl.BlockSpec((1, tq, near_w), lambda h: (h, 0, 0)),
            _whole((new_rows, V7X_LANES), 1), _whole((page_rows, V7X_LANES), 1),
            _whole((4 * V7X_SUBLANES, V7X_LANES), 1),
        ],
        out_shape=[
            jax.ShapeDtypeStruct((n_heads, tq, near_w), F32),
            jax.ShapeDtypeStruct((new_rows, V7X_LANES), F32),
            jax.ShapeDtypeStruct((page_rows, V7X_LANES), F32),
            jax.ShapeDtypeStruct((4 * V7X_SUBLANES, V7X_LANES), F32),
        ],
        compiler_params=_params("arbitrary"),
        name="bias_tables",
    )(rel_bias, rb_lanes)


def _gated_conv_chunk(xn, wu, wb, wc, wz, cw, t, halo0, halo1):
    u = _dot(xn, wu)
    gate_b = _dot(xn, wb)
    gate_c = _dot(xn, wc)
    z = _dot(xn, wz)
    cu = gate_c * u
    prev1 = jnp.where(t >= 1, pltpu.roll(cu, 1, 0), halo1)
    prev2 = jnp.where(t >= 2, pltpu.roll(cu, 2, 0), jnp.where(t == 1, halo1, halo0))
    conv = prev2 * cw[0:1] + prev1 * cw[1:2] + cu * cw[2:3]
    return cu, gate_b * conv * _silu(z)


def _conv_prompt_kernel(*refs, tiles_per_seq, from_input):
    if from_input:
        x_ref, meta_ref, *refs = refs
    else:
        h_ref, *refs = refs
    nw_ref, wu_ref, wb_ref, wc_ref, wz_ref, cw_ref, wo_ref, o_ref, tail_ref, xn_scr, carry_scr = refs
    i, j = pl.program_id(0), pl.program_id(1)
    tm, tn = o_ref.shape[0], wu_ref.shape[1]
    starts_seq = i % tiles_per_seq == 0

    def start_tile(h):
        xn_scr[...] = _rms_norm(h, nw_ref[...]).astype(BF16)
        o_ref[...] = h

    if from_input:
        @pl.when(jnp.logical_and(j == 0, starts_seq))
        def _():
            start_tile(jnp.concatenate([meta_ref[...], x_ref[0:tm - N_META, :]], axis=0))

        @pl.when(jnp.logical_and(j == 0, jnp.logical_not(starts_seq)))
        def _():
            start_tile(x_ref[...])
    else:
        @pl.when(j == 0)
        def _():
            start_tile(h_ref[...])

    @pl.when(jnp.logical_and(j == 0, starts_seq))
    def _():
        carry_scr[...] = jnp.zeros(carry_scr.shape, F32)

    t = lax.broadcasted_iota(jnp.int32, (tm, tn), 0)
    halo0 = carry_scr[j, V7X_SUBLANES - 2:V7X_SUBLANES - 1, :]
    halo1 = carry_scr[j, V7X_SUBLANES - 1:V7X_SUBLANES, :]
    cu, y = _gated_conv_chunk(xn_scr[...], wu_ref[...], wb_ref[...], wc_ref[...], wz_ref[...], cw_ref[...],
                              t, halo0, halo1)
    last_rows = cu[tm - V7X_SUBLANES:tm]
    carry_scr[j] = last_rows
    tail_ref[0] = last_rows
    o_ref[...] += _dot(y.astype(BF16), wo_ref[...])


def _conv_sample_kernel(h_ref, nw_ref, wu_ref, wb_ref, wc_ref, wz_ref, cw_ref, wo_ref, halo0_ref, halo1_ref,
                        o_ref, cu_ref, wu_out, wb_out, wc_out, wz_out, wo_out, xn_scr, *, seg_rows):
    j = pl.program_id(0)
    tm, tn = h_ref.shape[0], wu_ref.shape[1]

    @pl.when(j == 0)
    def _():
        h = h_ref[...]
        xn_scr[...] = _rms_norm(h, nw_ref[...]).astype(BF16)
        o_ref[...] = h

    wu, wb, wc, wz, wo = _to_bf16((wu_ref, wb_ref, wc_ref, wz_ref, wo_ref), (wu_out, wb_out, wc_out, wz_out, wo_out))
    t = _rem(lax.broadcasted_iota(jnp.int32, (tm, tn), 0), seg_rows)
    cu, y = _gated_conv_chunk(xn_scr[...], wu, wb, wc, wz, cw_ref[...], t, halo0_ref[...], halo1_ref[...])
    cu_ref[...] = cu
    o_ref[...] += _dot(y.astype(BF16), wo)


def _w_in_specs(layer, d, tn, col_map):
    n_chunks = d // tn
    return [
        pl.BlockSpec((None, d, tn), functools.partial(col_map, layer, c * n_chunks)) for c in range(4)
    ]


def _conv_layer_prompt(h, norm_w, w_in_groups, conv_w, w_out, layer, batch, seq_len):
    from_input = isinstance(h, tuple)
    d = w_out.shape[0]
    n_rows = batch * seq_len
    tm, tn = _row_tile(seq_len, ROW_TILE_CAP), PROMPT_COL_TILE
    tiles_per_seq = seq_len // tm
    n_tiles, n_chunks = n_rows // tm, d // tn
    if from_input:
        x_len = seq_len - N_META
        assert tm > N_META and tiles_per_seq * tm - N_META == x_len

        assert x_len % V7X_BF16_SUBLANES == 0 and N_META % V7X_BF16_SUBLANES == 0

        def x_window(i, j):
            row = (i // tiles_per_seq) * x_len + jnp.maximum((i % tiles_per_seq) * tm - N_META, 0)
            return pl.multiple_of(row, V7X_BF16_SUBLANES), 0

        h_specs = [pl.BlockSpec((pl.Element(tm), pl.Element(d)), x_window), _whole((N_META, d), 2)]
        h_args = list(h)
    else:
        h_specs = [pl.BlockSpec((tm, d), lambda i, j: (i, 0))]
        h_args = [h]
    w_chunk = pl.BlockSpec((d, tn), lambda i, j: (0, j))
    out, tails = pl.pallas_call(
        functools.partial(_conv_prompt_kernel, tiles_per_seq=tiles_per_seq, from_input=from_input),
        grid=(n_tiles, n_chunks),
        in_specs=[
            *h_specs,
            _whole((1, d), 2),
            w_chunk, w_chunk, w_chunk, w_chunk,
            pl.BlockSpec((None, CONV_WIDTH, tn), lambda i, j: (layer, 0, j)),
            pl.BlockSpec((tn, d), lambda i, j: (j, 0)),
        ],
        out_specs=[
            pl.BlockSpec((tm, d), lambda i, j: (i, 0)),
            pl.BlockSpec((1, V7X_SUBLANES, tn), lambda i, j: (i, 0, j)),
        ],
        out_shape=[
            jax.ShapeDtypeStruct((n_rows, d), F32),
            jax.ShapeDtypeStruct((n_tiles, V7X_SUBLANES, d), F32),
        ],
        scratch_shapes=[
            pltpu.VMEM((tm, d), BF16),
            pltpu.VMEM((n_chunks, V7X_SUBLANES, tn), F32),
        ],
        compiler_params=_params("arbitrary", "arbitrary"),
        name="conv_mixer_prompt",
    )(*h_args, norm_w, *w_in_groups, conv_w, w_out)
    state = tails.reshape(batch, tiles_per_seq, V7X_SUBLANES, d)[:, -1, V7X_SUBLANES - (CONV_WIDTH - 1):]
    return out, state


def _conv_layer_sample(h, norm_w, w_in, conv_w, w_out, layer, state, seq_len):
    n_rows, d = h.shape
    tn = COL_TILE
    n_chunks = d // tn
    n_seq = n_rows // seq_len
    halo0 = jnp.repeat(state[:, 0], seq_len, axis=0)
    halo1 = jnp.repeat(state[:, 1], seq_len, axis=0)
    rows_chunk = pl.BlockSpec((n_rows, tn), lambda j: (0, j))
    w_chunk = pl.BlockSpec((d, tn), lambda j: (0, j))
    w_bf16 = jax.ShapeDtypeStruct((d, d), BF16)
    out, cu, *weights = pl.pallas_call(
        functools.partial(_conv_sample_kernel, seg_rows=seq_len),
        grid=(n_chunks,),
        in_specs=[
            _whole((n_rows, d), 1),
            _whole((1, d), 1),
            *_w_in_specs(layer, d, tn, lambda l, c0, j: (l, 0, c0 + j)),
            pl.BlockSpec((None, CONV_WIDTH, tn), lambda j: (layer, 0, j)),
            pl.BlockSpec((None, tn, d), lambda j: (layer, j, 0)),
            rows_chunk, rows_chunk,
        ],
        out_specs=[
            _whole((n_rows, d), 1), rows_chunk,
            w_chunk, w_chunk, w_chunk, w_chunk,
            pl.BlockSpec((tn, d), lambda j: (j, 0)),
        ],
        out_shape=[
            jax.ShapeDtypeStruct((n_rows, d), F32),
            jax.ShapeDtypeStruct((n_rows, d), F32),
            w_bf16, w_bf16, w_bf16, w_bf16, w_bf16,
        ],
        scratch_shapes=[pltpu.VMEM((n_rows, d), BF16)],
        compiler_params=_params("arbitrary"),
        name="conv_mixer_sample",
    )(h, norm_w, w_in, w_in, w_in, w_in, conv_w, w_out, halo0, halo1)
    new_state = cu.reshape(n_seq, seq_len, d)[:, seq_len - (CONV_WIDTH - 1):]
    return out, new_state, weights[:4], weights[4]


def _attn_inproj_sample_kernel(h_ref, nw_ref, wq_ref, wk_ref, wv_ref, wz_ref,
                               q_ref, k_ref, v_ref, z_ref, wq_out, wk_out, wv_out, wz_out, xn_scr):
    @pl.when(pl.program_id(0) == 0)
    def _():
        xn_scr[...] = _rms_norm(h_ref[...], nw_ref[...]).astype(BF16)

    wq, wk, wv, wz = _to_bf16((wq_ref, wk_ref, wv_ref, wz_ref), (wq_out, wk_out, wv_out, wz_out))
    xn = xn_scr[...]
    q_ref[...] = _dot(xn, wq) * SCORE_SCALE
    k_ref[...] = _dot(xn, wk)
    v_ref[...] = _dot(xn, wv)
    z_ref[...] = _dot(xn, wz)


def _attn_inproj_sample(h, norm_w, w_in, layer):
    n_rows, d = h.shape
    tn = COL_TILE
    chunk = pl.BlockSpec((n_rows, tn), lambda j: (0, j))
    w_chunk = pl.BlockSpec((d, tn), lambda j: (0, j))
    outs = pl.pallas_call(
        _attn_inproj_sample_kernel,
        grid=(d // tn,),
        in_specs=[_whole((n_rows, d), 1), _whole((1, d), 1),
                  *_w_in_specs(layer, d, tn, lambda l, c0, j: (l, 0, c0 + j))],
        out_specs=[chunk] * 4 + [w_chunk] * 4,
        out_shape=[jax.ShapeDtypeStruct((n_rows, d), F32)] * 4 + [jax.ShapeDtypeStruct((d, d), BF16)] * 4,
        scratch_shapes=[pltpu.VMEM((n_rows, d), BF16)],
        compiler_params=_params("arbitrary"),
        name="attn_inproj_sample",
    )(h, norm_w, w_in, w_in, w_in, w_in)
    return outs[:4], outs[4:]


def _attn_inproj_prompt_kernel(h_ref, nw_ref, wq_ref, wk_ref, wv_ref, wz_ref, *rest, n_earlier):
    earlier, rest = rest[:2 * n_earlier], rest[2 * n_earlier:]
    q_ref, kb_ref, vb_ref, z_ref, k_out, v_out, xn_scr, chunk_buf, chunk_sem, *staging = rest
    i, j = pl.program_id(0), pl.program_id(1)
    n_tiles, n_chunks = pl.num_programs(0), pl.num_programs(1)
    tm = chunk_buf.shape[2]
    step = i * n_chunks + j
    last_step = step == n_tiles * n_chunks - 1
    slot = lax.rem(step, 2)

    def chunk_copies(at_step, buf_slot):
        rows = pl.ds(pl.multiple_of((at_step // n_chunks) * tm, V7X_SUBLANES), tm)
        heads_per_chunk = chunk_buf.shape[3] // HEAD_WIDTH
        for c, out in enumerate((k_out, v_out)):
            for hh in range(heads_per_chunk):
                head = lax.rem(at_step, n_chunks) * heads_per_chunk + hh
                src = chunk_buf.at[c, buf_slot, :, hh * HEAD_WIDTH:(hh + 1) * HEAD_WIDTH]
                dst = out.at[n_earlier, rows, head, :] if n_earlier else out.at[rows, head, :]
                yield pltpu.make_async_copy(src, dst, chunk_sem.at[c, buf_slot, hh])

    def slab_copies(slab, inward):
        stage, in_sem, out_sem = staging
        slab_rows = stage.shape[2]
        rows = pl.ds(slab * slab_rows, slab_rows)
        slab_slot = lax.rem(slab, 2)
        for a, src in enumerate(earlier):
            buf = stage.at[a, slab_slot]
            if inward:
                yield pltpu.make_async_copy(src.at[rows], buf, in_sem.at[a, slab_slot])
            else:
                yield pltpu.make_async_copy(buf, (k_out, v_out)[a % 2].at[a // 2, rows], out_sem.at[a, slab_slot])

    def start(copies):
        for copy in copies:
            copy.start(priority=1)

    def wait(copies):
        for copy in copies:
            copy.wait()

    if n_earlier:
        @pl.when(step == 0)
        def _():
            start(slab_copies(0, inward=True))

        @pl.when(step == 1)
        def _():
            wait(slab_copies(0, inward=True))
            start(slab_copies(0, inward=False))
            start(slab_copies(1, inward=True))

    @pl.when(step >= 2)
    def _():
        wait(chunk_copies(step - 2, slot))
        if n_earlier:
            wait(slab_copies(step - 2, inward=False))
            wait(slab_copies(step - 1, inward=True))
            start(slab_copies(step - 1, inward=False))
            start(slab_copies(step, inward=True))

    @pl.when(j == 0)
    def _():
        xn_scr[...] = _rms_norm(h_ref[...], nw_ref[...]).astype(BF16)

    xn = xn_scr[...]
    q_ref[...] = (_dot(xn, wq_ref[...]) * SCORE_SCALE).astype(BF16)
    z_ref[...] = _dot(xn, wz_ref[...]).astype(BF16)
    k = _dot(xn, wk_ref[...])
    v = _dot(xn, wv_ref[...])
    kb_ref[...] = k.astype(BF16)
    vb_ref[...] = v.astype(BF16)
    chunk_buf[0, slot] = k
    chunk_buf[1, slot] = v
    start(chunk_copies(step, slot))

    @pl.when(last_step)
    def _():
        wait(chunk_copies(step, slot))
        wait(chunk_copies(step - 1, 1 - slot))
        if n_earlier:
            wait(slab_copies(step, inward=True))
            start(slab_copies(step, inward=False))
            wait(slab_copies(step - 1, inward=False))
            wait(slab_copies(step, inward=False))


def _attn_inproj_prompt(h, norm_w, weights, tm, earlier_kv=()):
    n_rows, d = h.shape
    tn = PROMPT_COL_TILE
    assert tn % HEAD_WIDTH == 0
    n_heads = d // HEAD_WIDTH
    n_earlier = len(earlier_kv)
    n_steps = (n_rows // tm) * (d // tn)
    assert n_steps >= 2 and n_rows % n_steps == 0
    chunk = pl.BlockSpec((tm, tn), lambda i, j: (i, j))
    w_chunk = pl.BlockSpec((d, tn), lambda i, j: (0, j))
    in_hbm = pl.BlockSpec(memory_space=pl.ANY)
    act = jax.ShapeDtypeStruct((n_rows, d), BF16)
    kv_shape = (n_rows, n_heads, HEAD_WIDTH)
    scratch = [pltpu.VMEM((tm, d), BF16), pltpu.VMEM((2, 2, tm, tn), F32),
               pltpu.SemaphoreType.DMA((2, 2, tn // HEAD_WIDTH))]
    if n_earlier:
        kv_shape = (n_earlier + 1, *kv_shape)
        scratch += [
            pltpu.VMEM((2 * n_earlier, 2, n_rows // n_steps, n_heads, HEAD_WIDTH), F32),
            pltpu.SemaphoreType.DMA((2 * n_earlier, 2)),
            pltpu.SemaphoreType.DMA((2 * n_earlier, 2)),
        ]
    kv = jax.ShapeDtypeStruct(kv_shape, F32)
    return pl.pallas_call(
        functools.partial(_attn_inproj_prompt_kernel, n_earlier=n_earlier),
        grid=(n_rows // tm, d // tn),
        in_specs=[pl.BlockSpec((tm, d), lambda i, j: (i, 0)), _whole((1, d), 2), w_chunk, w_chunk, w_chunk, w_chunk]
        + [in_hbm] * (2 * n_earlier),
        out_specs=[chunk, chunk, chunk, chunk, in_hbm, in_hbm],
        out_shape=[act, act, act, act, kv, kv],
        scratch_shapes=scratch,
        compiler_params=_params("arbitrary", "arbitrary"),
        name="attn_inproj_prompt",
    )(h, norm_w, *weights, *[a for pair in earlier_kv for a in pair])


def _prompt_attn_kernel(rb_ref, q_ref, k_ref, v_ref, z_ref, near_ref, lam_ref, sw_ref,
                        y_ref, kb_scr, vb_scr, s_scr, a_scr, *, lam_init):
    head = pl.program_id(1)
    seq_len = q_ref.shape[1]
    padded_len = kb_scr.shape[0]
    tq = near_ref.shape[1]
    kb_scr[0:seq_len] = k_ref[0]
    vb_scr[0:seq_len] = v_ref[0]
    if padded_len > seq_len:
        kb_scr[seq_len:padded_len] = jnp.zeros((padded_len - seq_len, HEAD_WIDTH), BF16)
        vb_scr[seq_len:padded_len] = jnp.zeros((padded_len - seq_len, HEAD_WIDTH), BF16)
    lam = _lambda(lam_ref[...], lam_init)
    far_bias = rb_ref[N_BUCKETS - 1, head] * LOG2E

    for r0 in range(0, seq_len, tq):
        rows = min(tq, seq_len - r0)
        near_lo = max(r0 - MAX_DISTANCE, 0)
        near_hi = r0 + tq
        near_w = near_hi - near_lo
        col0 = near_lo - (r0 - MAX_DISTANCE)
        q = q_ref[0, r0:r0 + rows, :]
        row = lax.broadcasted_iota(jnp.int32, (rows, near_w), 0)
        col = lax.broadcasted_iota(jnp.int32, (rows, near_w), 1)
        visible = col + col0 <= row + MAX_DISTANCE
        near_bias = near_ref[0, 0:rows, col0:col0 + near_w]
        shifts, denom = [], []
        for c in range(2):
            lanes = slice(c * HEAD_DIM, (c + 1) * HEAD_DIM)
            s_near = _dot_nt(q[:, lanes], kb_scr[near_lo:near_hi, lanes])
            s_near = jnp.where(visible, s_near + near_bias, NEG_INF)
            s_scr[c, 0:rows, near_lo:near_hi] = s_near
            m = jnp.max(s_near, axis=-1, keepdims=True)
            if near_lo > 0:
                s_far = _dot_nt(q[:, lanes], kb_scr[0:near_lo, lanes])
                s_scr[c, 0:rows, 0:near_lo] = s_far
                m = jnp.maximum(m, jnp.max(s_far, axis=-1, keepdims=True) + far_bias)
            shifts.append(m)
        for c in range(2):
            m = shifts[c]
            e = jnp.exp2(s_scr[c, 0:rows, near_lo:near_hi] - m)
            total = jnp.sum(e, axis=-1, keepdims=True)
            s_scr[c, 0:rows, near_lo:near_hi] = e
            if near_lo > 0:
                e = jnp.exp2(s_scr[c, 0:rows, 0:near_lo] - (m - far_bias))
                total = total + jnp.sum(e, axis=-1, keepdims=True)
                s_scr[c, 0:rows, 0:near_lo] = e
            denom.append(total)
        w1 = 1.0 / denom[0]
        w2 = lam / denom[1]
        a_scr[0:rows, 0:near_hi] = (s_scr[0, 0:rows, 0:near_hi] * w1 - s_scr[1, 0:rows, 0:near_hi] * w2).astype(BF16)
        o = _dot(a_scr[0:rows, 0:near_hi], vb_scr[0:near_hi, :])
        z = z_ref[0, r0:r0 + rows, :].astype(F32)
        y_ref[0, r0:r0 + rows, :] = _sub_norm_gate(o, sw_ref[...], lam_init, z).astype(y_ref.dtype)


def _prompt_attention(q, k, v, z, rel_bias, near_table, lam_params, sub_w, lam_init):
    batch, seq_len, d = q.shape
    n_heads = d // HEAD_WIDTH
    tq = near_table.shape[1]
    padded_len = -(-seq_len // tq) * tq
    head_cols = pl.BlockSpec((1, seq_len, HEAD_WIDTH), lambda b, h: (b, 0, h))
    return pl.pallas_call(
        functools.partial(_prompt_attn_kernel, lam_init=lam_init),
        grid=(batch, n_heads),
        in_specs=[
            pl.BlockSpec(memory_space=pltpu.SMEM),
            head_cols, head_cols, head_cols, head_cols,
            pl.BlockSpec((1, tq, tq + MAX_DISTANCE), lambda b, h: (h, 0, 0)),
            _whole((4, HEAD_DIM), 2),
            _whole((1, HEAD_WIDTH), 2),
        ],
        out_specs=head_cols,
        out_shape=jax.ShapeDtypeStruct((batch, seq_len, d), BF16),
        scratch_shapes=[
            pltpu.VMEM((padded_len, HEAD_WIDTH), BF16),
            pltpu.VMEM((padded_len, HEAD_WIDTH), BF16),
            pltpu.VMEM((2, tq, padded_len), F32),
            pltpu.VMEM((tq, padded_len), BF16),
        ],
        compiler_params=_params("arbitrary", "arbitrary"),
        name="prompt_attention",
    )(rel_bias, q, k, v, z, near_table, lam_params, sub_w)


def _decode_attn_kernel(pt_ref, q_ref, kn_ref, vn_ref, z_ref, *rest, n_pg, lam_init):
    k_pages, v_pages = rest[:n_pg], rest[n_pg:2 * n_pg]
    (new_ref, tail_ref, far_ref, lam_ref, sw_ref, y_ref,
     s_scr, m_seen_scr, m_used_scr, l_scr, acc_scr) = rest[2 * n_pg:]
    del pt_ref
    step, n_groups = pl.program_id(1), pl.num_programs(1) - 1
    ls, d = z_ref.shape
    n_heads = d // HEAD_WIDTH
    n_q = n_heads * ls
    page_rows = PAGE_SIZE * n_heads
    tile = (V7X_SUBLANES, V7X_LANES)
    q_cols = q_ref[0]
    far_bias, far_bias_or_0, m_start, valid = far_ref[0:8], far_ref[8:16], far_ref[16:24], far_ref[24:32]
    slot = lax.rem(step, 2)

    def per_tile(x):
        return x.reshape(x.shape[0] // V7X_SUBLANES, *tile)

    def lanes_to_rows(stat):
        per_lane = jnp.sum(stat * valid, axis=0, keepdims=True)
        return jnp.broadcast_to(per_lane, tile).T[:, 0:1]

    def fold(probs, values, m_from, m_to):
        alpha = jnp.exp2(m_from - m_to)
        total = alpha * l_scr[...]
        pv = None
        for p, vb in zip(probs, values):
            total = total + jnp.sum(p, axis=0)
            part = lax.dot_general(p.reshape(vb.shape[0], V7X_LANES), vb, (((0,), (0,)), ((), ())),
                                   preferred_element_type=F32)
            pv = part if pv is None else pv + part
        l_scr[...] = total
        acc_scr[...] = lanes_to_rows(alpha) * acc_scr[...] + pv

    @pl.when(step == 0)
    def _():
        m_seen_scr[...] = m_start
        m_used_scr[...] = m_start
        l_scr[...] = jnp.zeros(l_scr.shape, F32)
        acc_scr[...] = jnp.zeros(acc_scr.shape, F32)
        s_scr[1] = jnp.full(s_scr.shape[1:], NEG_INF, F32)

    keys_are_tail = step >= n_groups - 1
    values_are_tail = step == n_groups

    m_use = m_seen_scr[...]
    prev = s_scr.at[1 - slot]
    probs = []
    for r in range(n_pg):
        s = per_tile(prev[r * page_rows:(r + 1) * page_rows])
        if r == n_pg - 1:
            probs.append(jnp.exp2(s + jnp.where(values_are_tail, per_tile(tail_ref[...]), far_bias) - m_use))
        else:
            probs.append(jnp.exp2(s + (far_bias - m_use)))
    fold(probs, [ref[0, 0].reshape(page_rows, HEAD_WIDTH) for ref in v_pages], m_used_scr[...], m_use)
    m_used_scr[...] = m_use

    m_new = m_use
    for r, ref in enumerate(k_pages):
        logits = _dot(ref[0, 0].reshape(page_rows, HEAD_WIDTH), q_cols)
        s_scr[slot, r * page_rows:(r + 1) * page_rows] = logits
        s = per_tile(logits)
        if r == n_pg - 1:
            s = s + jnp.where(keys_are_tail, per_tile(tail_ref[...]), far_bias_or_0)
            m_new = jnp.maximum(m_new, jnp.max(s, axis=0))
        else:
            m_new = jnp.maximum(m_new, jnp.max(s, axis=0) + far_bias_or_0)
    m_seen_scr[...] = m_new

    @pl.when(step == n_groups)
    def _():
        shape = new_ref.shape
        key = _div(lax.broadcasted_iota(jnp.int32, shape, 0), n_heads)
        query = _rem(lax.broadcasted_iota(jnp.int32, shape, 1), ls)
        s = per_tile(_dot(kn_ref[0], q_cols) + jnp.where(key <= query, new_ref[...], NEG_INF))
        m_all = jnp.maximum(m_use, jnp.max(s, axis=0))
        fold([jnp.exp2(s - m_all)], [vn_ref[0]], m_use, m_all)
        lam = _lambda(lam_ref[...], lam_init)
        o = acc_scr[...] / lanes_to_rows(l_scr[...])
        o = o[0:n_q] - lam * o[n_q:2 * n_q]
        o = o * lax.rsqrt(jnp.mean(o * o, axis=-1, keepdims=True) + EPS)
        o = o * sw_ref[...] * (1.0 - lam_init)
        for h in range(n_heads):
            cols = slice(h * HEAD_WIDTH, (h + 1) * HEAD_WIDTH)
            y_ref[:, cols] = o[h * ls:(h + 1) * ls] * _silu(z_ref[:, cols])


def _decode_attention(q, k_new, v_new, z, cache_k, cache_v, page_table, layer, new_table, tail_table, far_table,
                      lam_params, sub_w, lam_init, ls):
    n_rows, d = q.shape
    batch, n_pages = page_table.shape
    n_heads = d // HEAD_WIDTH
    n_q = n_heads * ls
    new_rows = new_table.shape[0]
    n_pg = math.gcd(n_pages, PAGES_PER_STEP)
    n_groups = n_pages // n_pg
    q_t = q.reshape(batch, ls, n_heads, 2, HEAD_DIM).transpose(0, 3, 4, 2, 1).reshape(batch, 2, HEAD_DIM, n_q)
    q_cols = jnp.concatenate([
        jnp.pad(q_t[:, c], ((0, 0), (0, 0), (c * n_q, V7X_LANES - (c + 1) * n_q))) for c in range(2)], axis=1)
    pad = ((0, 0), (0, new_rows - n_q), (0, 0))
    k_rows = jnp.pad(k_new.reshape(batch, n_q, HEAD_WIDTH), pad)
    v_rows = jnp.pad(v_new.reshape(batch, n_q, HEAD_WIDTH), pad)

    def page_specs(group_of_step):
        return [
            pl.BlockSpec((1, 1, PAGE_SIZE, n_heads, HEAD_WIDTH), functools.partial(
                lambda r, b, s, pt: (layer, pt[b, group_of_step(s) * n_pg + r], 0, 0, 0), r))
            for r in range(n_pg)
        ]

    new_page = pl.BlockSpec((1, new_rows, HEAD_WIDTH), lambda b, s, pt: (b, 0, 0))
    grid_spec = pltpu.PrefetchScalarGridSpec(
        num_scalar_prefetch=1,
        grid=(batch, n_groups + 1),
        in_specs=[
            pl.BlockSpec((1, HEAD_WIDTH, V7X_LANES), lambda b, s, pt: (b, 0, 0)),
            new_page, new_page,
            pl.BlockSpec((ls, d), lambda b, s, pt: (b, 0)),
            *page_specs(lambda s: jnp.minimum(s, n_groups - 1)), *page_specs(lambda s: jnp.maximum(s - 1, 0)),
            _whole(new_table.shape, 3), _whole(tail_table.shape, 3), _whole(far_table.shape, 3),
            _whole((4, HEAD_DIM), 3), _whole((1, HEAD_WIDTH), 3),
        ],
        out_specs=pl.BlockSpec((ls, d), lambda b, s, pt: (b, 0)),
        scratch_shapes=[
            pltpu.VMEM((2, n_pg * PAGE_SIZE * n_heads, V7X_LANES), F32),
            pltpu.VMEM((V7X_SUBLANES, V7X_LANES), F32),
            pltpu.VMEM((V7X_SUBLANES, V7X_LANES), F32),
            pltpu.VMEM((V7X_SUBLANES, V7X_LANES), F32),
            pltpu.VMEM((V7X_LANES, HEAD_WIDTH), F32),
        ],
    )
    return pl.pallas_call(
        functools.partial(_decode_attn_kernel, n_pg=n_pg, lam_init=lam_init),
        grid_spec=grid_spec,
        out_shape=jax.ShapeDtypeStruct((n_rows, d), F32),
        compiler_params=_params("arbitrary", "arbitrary"),
        name="decode_attention",
    )(page_table, q_cols, k_rows, v_rows, z, *([cache_k] * n_pg), *([cache_v] * n_pg),
      new_table, tail_table, far_table, lam_params, sub_w)


def _outproj_kernel(y_ref, w_ref, h_ref, *rest, final_norm):
    h = h_ref[...] + _dot(y_ref[...], w_ref[...])
    if final_norm:
        fw_ref, o_ref = rest
        o_ref[...] = _rms_norm(h, fw_ref[...])
    else:
        (o_ref,) = rest
        o_ref[...] = h


def _outproj_prompt(y, w_out, h, batch, seq_len, final_w=None):
    n_rows, d = h.shape
    w_spec = _whole((d, d), 1)
    if final_w is None:
        tm = _row_tile(seq_len, ROW_TILE_CAP)
        rows = pl.BlockSpec((tm, d), lambda i: (i, 0))
        return pl.pallas_call(
            functools.partial(_outproj_kernel, final_norm=False),
            grid=(n_rows // tm,),
            in_specs=[rows, w_spec, rows],
            out_specs=rows,
            out_shape=jax.ShapeDtypeStruct((n_rows, d), F32),
            compiler_params=_params("arbitrary"),
            name="attn_outproj",
        )(y, w_out, h)
    out_len = seq_len - N_META
    tm = _row_tile(out_len, ROW_TILE_CAP)
    tiles_per_seq = out_len // tm
    assert seq_len % V7X_BF16_SUBLANES == 0 and N_META % V7X_BF16_SUBLANES == 0

    def rows_window(i):
        row = (i // tiles_per_seq) * seq_len + N_META + (i % tiles_per_seq) * tm
        return pl.multiple_of(row, V7X_BF16_SUBLANES), 0

    window = pl.BlockSpec((pl.Element(tm), pl.Element(d)), rows_window)
    return pl.pallas_call(
        functools.partial(_outproj_kernel, final_norm=True),
        grid=(batch * tiles_per_seq,),
        in_specs=[window, w_spec, window, _whole((1, d), 1)],
        out_specs=pl.BlockSpec((tm, d), lambda i: (i, 0)),
        out_shape=jax.ShapeDtypeStruct((batch * out_len, d), F32),
        compiler_params=_params("arbitrary"),
        name="attn_outproj_final",
    )(y, w_out, h, final_w)


def _outproj_sample_kernel(y_ref, w_ref, h_ref, o_ref, w_out):
    (w,) = _to_bf16((w_ref,), (w_out,))
    o_ref[...] = h_ref[...] + _dot(y_ref[...].astype(BF16), w)


def _outproj_sample(y, w_out, layer, h):
    n_rows, d = h.shape
    tn = OUT_COL_TILE
    cols = pl.BlockSpec((n_rows, tn), lambda j: (0, j))
    return pl.pallas_call(
        _outproj_sample_kernel,
        grid=(d // tn,),
        in_specs=[_whole((n_rows, d), 1), pl.BlockSpec((None, d, tn), lambda j: (layer, 0, j)), cols],
        out_specs=[cols, pl.BlockSpec((d, tn), lambda j: (0, j))],
        out_shape=[jax.ShapeDtypeStruct((n_rows, d), F32), jax.ShapeDtypeStruct((d, d), BF16)],
        compiler_params=_params("arbitrary"),
        name="attn_outproj_sample",
    )(y, w_out, h)


def _final_norm_kernel(h_ref, w_ref, o_ref):
    o_ref[...] = _rms_norm(h_ref[...], w_ref[...])


def _final_norm(h, w):
    return pl.pallas_call(
        _final_norm_kernel, out_shape=jax.ShapeDtypeStruct(h.shape, F32), name="final_norm_sample")(h, w)


def kernel(x_prompt, x_sample, state_conv, cache_k, cache_v, page_table, meta_tokens, rel_bias, norm_w,
           final_norm_w, conv_w_in, conv_w, conv_w_out, attn_w_in, attn_lambda, attn_subln_w, attn_w_out):
    b, seq, d = x_prompt.shape
    db, ls, _ = x_sample.shape
    depth = norm_w.shape[0]
    lp = N_META + seq
    n_heads = d // HEAD_WIDTH
    assert d % HEAD_WIDTH == 0 and d % COL_TILE == 0 and d % OUT_COL_TILE == 0
    assert ls >= CONV_WIDTH - 1 and (db * ls) % V7X_BF16_SUBLANES == 0
    assert cache_k.shape[2:] == (PAGE_SIZE, n_heads, HEAD_WIDTH) and cache_v.shape == cache_k.shape
    assert ls <= PAGE_SIZE
    assert depth % 2 == 0, "the final norm is fused into the last attention mixer's out-projection"

    hp = (x_prompt.reshape(b * seq, d), meta_tokens.astype(x_prompt.dtype))
    hs = x_sample.reshape(db * ls, d)
    tm_p = _row_tile(lp, ROW_TILE_CAP)
    fw = final_norm_w.reshape(1, d)

    near_table, new_table, tail_table, far_table = _bias_tables(rel_bias, Q_TILE, ls)

    conv_p, conv_s, k_s, v_s = [], [], [], []
    kv_p = []
    for i in range(depth):
        nw = norm_w[i].reshape(1, d)
        j = i // 2
        last = i == depth - 1
        if i % 2 == 0:
            hs, st_s, w_in_b, w_out_b = _conv_layer_sample(hs, nw, conv_w_in, conv_w, conv_w_out, j, state_conv[j], ls)
            hp, st_p = _conv_layer_prompt(hp, nw, w_in_b, conv_w, w_out_b, j, b, lp)
            conv_p.append(st_p)
            conv_s.append(st_s)
        else:
            lam_init = 0.8 - 0.6 * math.exp(-0.3 * i)
            sub_w = attn_subln_w[j].reshape(1, HEAD_WIDTH)
            (qs, ks, vs, zs), w_in_b = _attn_inproj_sample(hs, nw, attn_w_in, j)
            ys = _decode_attention(qs, ks, vs, zs, cache_k, cache_v, page_table, j, new_table, tail_table,
                                   far_table, attn_lambda[j], sub_w, lam_init, ls)
            hs, w_out_b = _outproj_sample(ys, attn_w_out, j, hs)
            last_attn = j == depth // 2 - 1
            q, k, v, z, k_out, v_out = _attn_inproj_prompt(hp, nw, w_in_b, tm_p, kv_p if last_attn else ())
            kv_p = [(k_out, v_out)] if last_attn else kv_p + [(k_out, v_out)]
            yp = _prompt_attention(q.reshape(b, lp, d), k.reshape(b, lp, d), v.reshape(b, lp, d),
                                   z.reshape(b, lp, d), rel_bias, near_table, attn_lambda[j], sub_w, lam_init)
            hp = _outproj_prompt(yp.reshape(b * lp, d), w_out_b, hp, b, lp, fw if last else None)
            k_s.append(ks.reshape(db, ls, n_heads, HEAD_WIDTH))
            v_s.append(vs.reshape(db, ls, n_heads, HEAD_WIDTH))

    y_prompt = hp.reshape(b, seq, d)
    y_sample = _final_norm(hs, fw).reshape(db, ls, d)
    k_p, v_p = (a.reshape(depth // 2, b, lp, n_heads, HEAD_WIDTH) for a in kv_p[0])
    return (y_prompt, y_sample, jnp.stack(conv_p), jnp.stack(conv_s), k_p, v_p, jnp.stack(k_s), jnp.stack(v_s))
```
